```python
import jax, jax.numpy as jnp
from jax import lax
import numpy as np

D_MODEL = 4096
BATCH = 2
SEQ = 8192
DEPTH = 1

HEAD_DIM = 128
DILATED_GROUPS = ((128, 1), (512, 4), (2048, 16))
N_GROUPS = len(DILATED_GROUPS)
HEADS_PER_GROUP = D_MODEL // 512
N_HEADS = HEADS_PER_GROUP * N_GROUPS
QKV_WIDTH = N_HEADS * HEAD_DIM
ATTN_WIDTH = HEADS_PER_GROUP * HEAD_DIM
FOURIER_WIDTH = D_MODEL // 2
FOURIER_GROUPS = 8
FOURIER_GROUP_DIM = FOURIER_WIDTH // FOURIER_GROUPS
N_BRANCHES = 2
ROPE_THETA = 10000.0
NORM_EPS = 1e-6
NEG_INF = -1e30
IN_SPLITS = (QKV_WIDTH, QKV_WIDTH, QKV_WIDTH, ATTN_WIDTH, FOURIER_WIDTH, FOURIER_WIDTH, N_BRANCHES * D_MODEL)
IN_WIDTH = sum(IN_SPLITS)

kernel_name = "hybrid_dilated_attn_fnet_gated_block"


def rms_norm(x, gain):
    xf = x.astype(jnp.float32)
    y = xf * lax.rsqrt(jnp.mean(xf * xf, axis=-1, keepdims=True) + NORM_EPS)
    return (y * gain.astype(jnp.float32)).astype(x.dtype)


def rotary(t, positions):
    half = t.shape[-1] // 2
    inv_freq = ROPE_THETA ** (-jnp.arange(half, dtype=jnp.float32) * (2.0 / t.shape[-1]))
    ang = positions.astype(jnp.float32)[:, None] * inv_freq[None, :]
    cos = jnp.cos(ang)[None, :, None, :]
    sin = jnp.sin(ang)[None, :, None, :]
    t1, t2 = t[..., :half], t[..., half:]
    return jnp.concatenate([t1 * cos - t2 * sin, t2 * cos + t1 * sin], axis=-1)


def banded_attention(q, k, v, radius):
    n, L, h, dh = q.shape
    blk = radius
    nb = -(-L // blk)
    pad = nb * blk - L
    q = jnp.pad(q, ((0, 0), (0, pad), (0, 0), (0, 0))).reshape(n, nb, blk, h, dh)
    kv_pad = ((0, 0), (blk, pad + blk), (0, 0), (0, 0))
    k = jnp.pad(k, kv_pad).reshape(n, nb + 2, blk, h, dh)
    v = jnp.pad(v, kv_pad).reshape(n, nb + 2, blk, h, dh)

    def neighbours(t):
        return jnp.concatenate([t[:, :-2], t[:, 1:-1], t[:, 2:]], axis=2)

    kb, vb = neighbours(k), neighbours(v)
    s = jnp.einsum('nbqhd,nbkhd->nbhqk', q, kb) * (dh ** -0.5)
    qpos = jnp.arange(nb)[:, None] * blk + jnp.arange(blk)[None, :]
    kpos = (jnp.arange(nb)[:, None] - 1) * blk + jnp.arange(3 * blk)[None, :]
    rel = kpos[:, None, :] - qpos[:, :, None]
    valid = (jnp.abs(rel) <= radius) & (kpos[:, None, :] >= 0) & (kpos[:, None, :] < L)
    s = jnp.where(valid[None, :, None], s, NEG_INF)
    lse = jax.nn.logsumexp(s, axis=-1)
    p = jnp.exp(s - lse[..., None])
    o = jnp.einsum('nbhqk,nbkhd->nbqhd', p, vb).reshape(n, nb * blk, h, dh)[:, :L]
    lse = lse.transpose(0, 1, 3, 2).reshape(n, nb * blk, h)[:, :L]
    return o, lse


def dilated_attention(q, k, v, window, dilation):
    b, s, h, dh = q.shape
    sub = s // dilation

    def to_classes(t):
        return t.reshape(b, sub, dilation, h, dh).transpose(0, 2, 1, 3, 4).reshape(b * dilation, sub, h, dh)

    o, lse = banded_attention(to_classes(q), to_classes(k), to_classes(v), window // (2 * dilation))
    o = o.reshape(b, dilation, sub, h, dh).transpose(0, 2, 1, 3, 4).reshape(b, s, h, dh)
    lse = lse.reshape(b, dilation, sub, h).transpose(0, 2, 1, 3).reshape(b, s, h)
    return o, lse


def setup_inputs(seed: int = 0) -> dict:
    key = jax.random.key(seed)
    ks = jax.random.split(key, 9)
    f32 = jnp.float32
    x = jax.random.normal(ks[0], (BATCH, SEQ, D_MODEL), f32)
    norm_gain = 1.0 + 0.02 * jax.random.normal(ks[1], (DEPTH, D_MODEL), f32)
    w_in = jax.random.normal(ks[2], (DEPTH, D_MODEL, IN_WIDTH), f32) * D_MODEL ** -0.5
    gate_bias = 0.01 * jax.random.normal(ks[3], (DEPTH, N_BRANCHES * D_MODEL), f32)
    w_branch_attn = jax.random.normal(ks[4], (DEPTH, ATTN_WIDTH, D_MODEL), f32) * ATTN_WIDTH ** -0.5
    w_branch_fourier = jax.random.normal(ks[5], (DEPTH, FOURIER_WIDTH, D_MODEL), f32) * FOURIER_WIDTH ** -0.5
    w_out = jax.random.normal(ks[6], (DEPTH, D_MODEL, D_MODEL), f32) * D_MODEL ** -0.5
    final_norm_gain = 1.0 + 0.02 * jax.random.normal(ks[7], (D_MODEL,), f32)
    return {"x": x, "norm_gain": norm_gain, "w_in": w_in, "gate_bias": gate_bias,
            "w_branch_attn": w_branch_attn, "w_branch_fourier": w_branch_fourier,
            "w_out": w_out, "final_norm_gain": final_norm_gain}


def reference(x, norm_gain, w_in, gate_bias, w_branch_attn, w_branch_fourier, w_out, final_norm_gain):
    b, s, _ = x.shape
    dtype = x.dtype
    positions = jnp.arange(s, dtype=jnp.int32)
    offsets = np.cumsum(IN_SPLITS)[:-1].tolist()
    for layer in range(DEPTH):
        h = rms_norm(x, norm_gain[layer])
        proj = h @ w_in[layer]
        q, k, v, z_attn, u_four, z_four, gates = jnp.split(proj, offsets, axis=-1)

        q = rotary(q.astype(jnp.float32).reshape(b, s, N_HEADS, HEAD_DIM), positions)
        k = rotary(k.astype(jnp.float32).reshape(b, s, N_HEADS, HEAD_DIM), positions)
        v = v.astype(jnp.float32).reshape(b, s, N_HEADS, HEAD_DIM)
        outs, lses = [], []
        for g, (window, dilation) in enumerate(DILATED_GROUPS):
            hs = slice(g * HEADS_PER_GROUP, (g + 1) * HEADS_PER_GROUP)
            o_g, lse_g = dilated_attention(q[:, :, hs], k[:, :, hs], v[:, :, hs], window, dilation)
            outs.append(o_g)
            lses.append(lse_g)
        outs = jnp.stack(outs, axis=0)
        alpha = jax.nn.softmax(jnp.stack(lses, axis=0), axis=0)
        o_attn = jnp.sum(alpha[..., None] * outs, axis=0).reshape(b, s, ATTN_WIDTH).astype(dtype)
        branch_attn = (o_attn * jax.nn.silu(z_attn)) @ w_branch_attn[layer]

        u = u_four.astype(jnp.float32).reshape(b, s, FOURIER_GROUPS, FOURIER_GROUP_DIM)
        y_four = jnp.fft.fft2(u, axes=(1, 3), norm="ortho").real.reshape(b, s, FOURIER_WIDTH).astype(dtype)
        branch_four = (y_four * jax.nn.silu(z_four)) @ w_branch_fourier[layer]

        gate_logits = gates + gate_bias[layer]
        g_attn, g_four = jnp.split(gate_logits, N_BRANCHES, axis=-1)
        mixed = jax.nn.sigmoid(g_attn) * branch_attn + jax.nn.sigmoid(g_four) * branch_four
        x = x + mixed @ w_out[layer]
    return rms_norm(x, final_norm_gain)
```

```python
import functools

import numpy as np
import jax
import jax.numpy as jnp
from jax import lax
from jax.experimental import pallas as pl
from jax.experimental.pallas import tpu as pltpu

HEAD_DIM = 128
DILATED_GROUPS = ((128, 1), (512, 4), (2048, 16))
N_GROUPS = len(DILATED_GROUPS)
FOURIER_GROUPS = 8
ROPE_THETA = 10000.0
NORM_EPS = 1e-6
NEG_INF = -1e30
ATTN_RADIUS = 64
FFT_P = 16
FFT_F2_BLOCK = 16
LANES = 128
VMEM_LIMIT = 56 * 1024 * 1024

F32 = jnp.float32
BF16 = jnp.bfloat16


def _params(sem, vmem=VMEM_LIMIT):
    return pltpu.CompilerParams(dimension_semantics=sem, vmem_limit_bytes=vmem)


def _rmsnorm_kernel(x_ref, g_ref, *refs, dilations):
    o_ref = refs[0]
    x = x_ref[...].astype(F32)
    ms = jnp.mean(x * x, axis=-1, keepdims=True)
    y = x * lax.rsqrt(ms + NORM_EPS) * g_ref[...]
    o_ref[...] = y.astype(o_ref.dtype)
    if dilations:
        ybuf = refs[-1]
        rows = y.shape[0]
        chunks = [slice(c * LANES, (c + 1) * LANES) for c in range(y.shape[1] // LANES)]
        for c, sl in enumerate(chunks):
            ybuf[c] = y[:, sl]
        for d, od_ref in zip(dilations, refs[1:-1]):
            for r in range(d):
                for c, sl in enumerate(chunks):
                    od_ref[r, :, sl] = ybuf[c, pl.ds(r, rows // d, stride=d), :].astype(od_ref.dtype)


def _rmsnorm(x, gain, out_dtype, dilations=(), tm=512):
    batch, seq, d = x.shape
    out_specs = [pl.BlockSpec((None, tm, d), lambda b, i: (b, i, 0))]
    out_shape = [jax.ShapeDtypeStruct((batch, seq, d), out_dtype)]
    for dil in dilations:
        out_specs.append(pl.BlockSpec((None, dil, tm // dil, d), lambda b, i: (b, 0, i, 0)))
        out_shape.append(jax.ShapeDtypeStruct((batch, dil, seq // dil, d), out_dtype))
    return pl.pallas_call(
        functools.partial(_rmsnorm_kernel, dilations=tuple(dilations)),
        grid=(batch, seq // tm),
        in_specs=[pl.BlockSpec((None, tm, d), lambda b, i: (b, i, 0)),
                  pl.BlockSpec((1, d), lambda b, i: (0, 0))],
        out_specs=out_specs,
        out_shape=out_shape,
        scratch_shapes=[pltpu.VMEM((d // LANES, tm, LANES), F32)] if dilations else [],
        compiler_params=_params(("parallel", "parallel")),
        name="rmsnorm",
    )(x, gain.reshape(1, d).astype(F32))


def _inproj_kernel(tiles_ref, h_ref, w_ref, cos_ref, sin_ref, o_ref, *, rot_lo):
    del tiles_ref
    j = pl.program_id(1)
    acc = jnp.dot(h_ref[...], w_ref[...], preferred_element_type=F32)
    is_rot = (j >= rot_lo) & (j < rot_lo + 2)

    @pl.when(is_rot)
    def _():
        c = cos_ref[...]
        s = sin_ref[...]
        for hd in range(acc.shape[1] // HEAD_DIM):
            sl = slice(hd * HEAD_DIM, (hd + 1) * HEAD_DIM)
            xh = acc[:, sl]
            o_ref[:, sl] = (xh * c + pltpu.roll(xh, HEAD_DIM // 2, axis=1) * s).astype(o_ref.dtype)

    @pl.when(jnp.logical_not(is_rot))
    def _():
        o_ref[...] = acc.astype(o_ref.dtype)


def _inproj(h, w, col_tiles, rot_lo, cos_tab, sin_tab, seq, tm=1024, tn=1024):
    m, k = h.shape
    seq_tiles = seq // tm
    tab_spec = pl.BlockSpec((None, tm, HEAD_DIM),
                            lambda i, j, t: (jnp.where(j == rot_lo, 0, 1), i % seq_tiles, 0))
    grid_spec = pltpu.PrefetchScalarGridSpec(
        num_scalar_prefetch=1,
        grid=(m // tm, len(col_tiles)),
        in_specs=[pl.BlockSpec((tm, k), lambda i, j, t: (i, 0)),
                  pl.BlockSpec((k, tn), lambda i, j, t: (0, t[j])),
                  tab_spec, tab_spec],
        out_specs=pl.BlockSpec((tm, tn), lambda i, j, t: (i, j)),
    )
    return pl.pallas_call(
        functools.partial(_inproj_kernel, rot_lo=rot_lo),
        grid_spec=grid_spec,
        out_shape=jax.ShapeDtypeStruct((m, len(col_tiles) * tn), BF16),
        compiler_params=_params(("parallel", "arbitrary")),
        name="inproj",
    )(jnp.asarray(np.asarray(col_tiles, np.int32)), h, w, cos_tab, sin_tab)


def _attn_kernel(q_ref, kp_ref, kc_ref, kn_ref, vp_ref, vc_ref, vn_ref, o_ref, lse_ref,
                 kbuf, vbuf, *, tq, sub, n_heads):
    it = pl.program_id(1)
    r = ATTN_RADIUS
    qb = 2 * r
    kb = 4 * r
    kbuf[0:r, :] = kp_ref[...]
    kbuf[r:r + tq, :] = kc_ref[...]
    kbuf[r + tq:, :] = kn_ref[...]
    vbuf[0:r, :] = vp_ref[...]
    vbuf[r:r + tq, :] = vc_ref[...]
    vbuf[r + tq:, :] = vn_ref[...]

    def body(i, carry):
        r0 = pl.multiple_of(i * qb, qb)
        row = lax.broadcasted_iota(jnp.int32, (qb, kb), 0)
        col = lax.broadcasted_iota(jnp.int32, (qb, kb), 1)
        delta = col - row
        kpos = col + (it * tq + i * qb - r)
        valid = (delta >= 0) & (delta <= 2 * r) & (kpos >= 0) & (kpos < sub)
        lane = lax.broadcasted_iota(jnp.int32, (qb, LANES), 1)
        lse_tile = jnp.zeros((qb, LANES), F32)
        for h in range(n_heads):
            sl = slice(h * HEAD_DIM, (h + 1) * HEAD_DIM)
            q = q_ref[pl.ds(r0, qb), sl]
            k = kbuf[pl.ds(r0, kb), sl]
            v = vbuf[pl.ds(r0, kb), sl]
            s = lax.dot_general(q, k, (((1,), (1,)), ((), ())), preferred_element_type=F32)
            s = jnp.where(valid, s, NEG_INF)
            m = jnp.max(s, axis=-1, keepdims=True)
            p = jnp.exp(s - m)
            l = jnp.sum(p, axis=-1, keepdims=True)
            o = jnp.dot(p.astype(BF16), v, preferred_element_type=F32)
            o_ref[pl.ds(r0, qb), sl] = (o / l).astype(o_ref.dtype)
            lse_tile = jnp.where(lane == h, m + jnp.log(l), lse_tile)
        lse_ref[pl.ds(r0, qb), :] = lse_tile
        return carry

    lax.fori_loop(0, tq // qb, body, 0)


def _banded_attention(qkv, q_tile, n_heads, tq=512):
    n, sub, _ = qkv.shape
    width = n_heads * HEAD_DIM
    tq = min(tq, sub)
    r = ATTN_RADIUS
    per = tq // r

    def cur(off):
        return pl.BlockSpec((None, tq, width), lambda b, t: (b, t, q_tile + off))

    def prev(off):
        return pl.BlockSpec((None, r, width), lambda b, t: (b, jnp.maximum(t * per - 1, 0), q_tile + off))

    def nxt(off):
        return pl.BlockSpec((None, r, width),
                            lambda b, t: (b, jnp.minimum((t + 1) * per, sub // r - 1), q_tile + off))

    return pl.pallas_call(
        functools.partial(_attn_kernel, tq=tq, sub=sub, n_heads=n_heads),
        grid=(n, sub // tq),
        in_specs=[cur(0), prev(1), cur(1), nxt(1), prev(2), cur(2), nxt(2)],
        out_specs=[pl.BlockSpec((None, tq, width), lambda b, t: (b, t, 0)),
                   pl.BlockSpec((None, tq, LANES), lambda b, t: (b, t, 0))],
        out_shape=[jax.ShapeDtypeStruct((n, sub, width), BF16),
                   jax.ShapeDtypeStruct((n, sub, LANES), F32)],
        scratch_shapes=[pltpu.VMEM((tq + 2 * r, width), BF16),
                        pltpu.VMEM((tq + 2 * r, width), BF16)],
        compiler_params=_params(("parallel", "parallel")),
        name=f"attn_sub{sub}",
    )(qkv, qkv, qkv, qkv, qkv, qkv, qkv)


def _dft_tables(seq, gdim):
    p, fb = FFT_P, FFT_F2_BLOCK
    q = seq // p
    c = np.arange(gdim, dtype=np.float64)
    ang = 2 * np.pi * np.outer(c, c) / gdim
    wc = np.concatenate([np.cos(ang), np.sin(ang)], axis=1) / np.sqrt(gdim)
    aq = 2 * np.pi * (np.outer(np.arange(q), np.arange(q)) % q) / q
    cq, sq = np.cos(aq) / np.sqrt(q), np.sin(aq) / np.sqrt(q)
    w1 = np.block([[cq, -sq], [-sq, -cq]])
    f = (q * np.arange(p)[None, :, None, None]
         + fb * np.arange(q // fb)[:, None, None, None] + np.arange(fb)[None, None, :, None])
    phi = 2 * np.pi * ((f * np.arange(p)[None, None, None, :]) % seq) / seq
    eye = np.eye(fb)
    parts = [np.einsum('afjs,jk->afjsk', t / np.sqrt(p), eye) for t in (np.cos(phi), np.sin(phi))]
    big = np.stack(parts, axis=3).reshape(q // fb, p * fb, 2 * p * fb)
    as_bf16 = lambda a: jnp.asarray(a.astype(np.float32)).astype(BF16)
    return as_bf16(wc), as_bf16(w1), as_bf16(big)


def _fft1_kernel(u_ref, wc_ref, w1_ref, h_ref, *, gdim):
    q = u_ref.shape[0]
    for g in range(u_ref.shape[1] // gdim):
        sl = slice(g * gdim, (g + 1) * gdim)
        ab = jnp.dot(u_ref[:, sl], wc_ref[...], preferred_element_type=F32)
        stack = jnp.concatenate([ab[:, :gdim], ab[:, gdim:]], axis=0).astype(BF16)
        hh = jnp.dot(w1_ref[...], stack, preferred_element_type=F32)
        h_ref[0, :, sl] = hh[:q].astype(h_ref.dtype)
        h_ref[1, :, sl] = hh[q:].astype(h_ref.dtype)


def _fft2_kernel(h_ref, big_ref, o_ref):
    parts, p, fb, width = h_ref.shape
    rhs = h_ref[...].reshape(parts * p * fb, width)
    y = jnp.dot(big_ref[...], rhs, preferred_element_type=F32)
    o_ref[...] = y.reshape(p, fb, width).astype(o_ref.dtype)


def _fourier_mix(u_cm, u_tile, batch, seq, width):
    gdim = width // FOURIER_GROUPS
    p, fb = FFT_P, FFT_F2_BLOCK
    q = seq // p
    wc, w1, big = _dft_tables(seq, gdim)
    h = pl.pallas_call(
        functools.partial(_fft1_kernel, gdim=gdim),
        grid=(batch, p),
        in_specs=[pl.BlockSpec((None, None, q, width), lambda b, s: (b, s, 0, u_tile)),
                  pl.BlockSpec(wc.shape, lambda b, s: (0, 0)),
                  pl.BlockSpec(w1.shape, lambda b, s: (0, 0))],
        out_specs=pl.BlockSpec((None, 2, None, q, width), lambda b, s: (b, 0, s, 0, 0)),
        out_shape=jax.ShapeDtypeStruct((batch, 2, p, q, width), BF16),
        compiler_params=_params(("parallel", "parallel")),
        name="fft_seq1",
    )(u_cm, wc, w1)
    y = pl.pallas_call(
        _fft2_kernel,
        grid=(batch, q // fb),
        in_specs=[pl.BlockSpec((None, 2, p, fb, width), lambda b, a: (b, 0, 0, a, 0)),
                  pl.BlockSpec((None, p * fb, 2 * p * fb), lambda b, a: (a, 0, 0))],
        out_specs=pl.BlockSpec((None, p, None, fb, width), lambda b, a: (b, 0, a, 0, 0)),
        out_shape=jax.ShapeDtypeStruct((batch, p, q // fb, fb, width), BF16),
        compiler_params=_params(("parallel", "parallel")),
        name="fft_seq2",
    )(h, big)
    return y.reshape(batch, seq, width)


def _silu(z):
    return z * jax.nn.sigmoid(z)


def _branch_kernel(o1_ref, o2_ref, o3_ref, l1_ref, l2_ref, l3_ref, za_ref, y_ref, zf_ref,
                   wa_ref, wf_ref, ga_ref, gf_ref, ba_ref, bf_ref, out_ref,
                   a_attn, a_four, on2, on3, ln2, ln3, *, n_heads):
    @pl.when(pl.program_id(2) == 0)
    def _():
        rows = ln2.shape[0]
        for src, dst in ((o2_ref, on2), (o3_ref, on3)):
            d = src.shape[0]
            for r in range(d):
                for h in range(n_heads):
                    dst[h, pl.ds(r, rows // d, stride=d), :] = (
                        src[r, :, h * HEAD_DIM:(h + 1) * HEAD_DIM].astype(F32))
        for src, dst in ((l2_ref, ln2), (l3_ref, ln3)):
            d = src.shape[0]
            for r in range(d):
                dst[pl.ds(r, rows // d, stride=d), :] = src[r]
        l1, l2, l3 = l1_ref[...], ln2[...], ln3[...]
        m = jnp.maximum(jnp.maximum(l1, l2), l3)
        e1, e2, e3 = jnp.exp(l1 - m), jnp.exp(l2 - m), jnp.exp(l3 - m)
        inv = 1.0 / (e1 + e2 + e3)
        al1, al2, al3 = e1 * inv, e2 * inv, e3 * inv
        for h in range(n_heads):
            sl = slice(h * HEAD_DIM, (h + 1) * HEAD_DIM)
            o = (al1[:, h:h + 1] * o1_ref[:, sl].astype(F32)
                 + al2[:, h:h + 1] * on2[h]
                 + al3[:, h:h + 1] * on3[h])
            a_attn[:, sl] = (o * _silu(za_ref[:, sl].astype(F32))).astype(a_attn.dtype)
        a_four[...] = (y_ref[...].astype(F32) * _silu(zf_ref[...].astype(F32))).astype(a_four.dtype)

    br_a = jnp.dot(a_attn[...], wa_ref[...], preferred_element_type=F32)
    br_f = jnp.dot(a_four[...], wf_ref[...], preferred_element_type=F32)
    g_a = jax.nn.sigmoid(ga_ref[...].astype(F32) + ba_ref[...])
    g_f = jax.nn.sigmoid(gf_ref[...].astype(F32) + bf_ref[...])
    out_ref[...] = (g_a * br_a + g_f * br_f).astype(out_ref.dtype)


def _branches(proj, outs, lses, y_four, w_attn, w_four, gate_bias, za_tile, zf_tile, gate_tile,
              n_heads, tm=512, tn=1024):
    batch, seq, _ = proj.shape
    aw, fw = w_attn.shape[0], w_four.shape[0]
    d = w_attn.shape[1]
    nj = d // tn
    bias = gate_bias.reshape(1, 2 * d).astype(F32)

    def nat(width, tile=0):
        return pl.BlockSpec((None, tm, width), lambda b, i, j: (b, i, tile))

    def cm(arr):
        dil = arr.shape[1]
        return pl.BlockSpec((None, dil, tm // dil, arr.shape[3]), lambda b, i, j: (b, 0, i, 0))

    return pl.pallas_call(
        functools.partial(_branch_kernel, n_heads=n_heads),
        grid=(batch, seq // tm, nj),
        in_specs=[nat(aw), cm(outs[1]), cm(outs[2]), nat(LANES), cm(lses[1]), cm(lses[2]),
                  nat(aw, za_tile), nat(fw), nat(fw, zf_tile),
                  pl.BlockSpec((aw, tn), lambda b, i, j: (0, j)),
                  pl.BlockSpec((fw, tn), lambda b, i, j: (0, j)),
                  pl.BlockSpec((None, tm, tn), lambda b, i, j: (b, i, gate_tile + j)),
                  pl.BlockSpec((None, tm, tn), lambda b, i, j: (b, i, gate_tile + nj + j)),
                  pl.BlockSpec((1, tn), lambda b, i, j: (0, j)),
                  pl.BlockSpec((1, tn), lambda b, i, j: (0, nj + j))],
        out_specs=pl.BlockSpec((None, tm, tn), lambda b, i, j: (b, i, j)),
        out_shape=jax.ShapeDtypeStruct((batch, seq, d), BF16),
        scratch_shapes=[pltpu.VMEM((tm, aw), BF16), pltpu.VMEM((tm, fw), BF16),
                        pltpu.VMEM((n_heads, tm, HEAD_DIM), F32), pltpu.VMEM((n_heads, tm, HEAD_DIM), F32),
                        pltpu.VMEM((tm, LANES), F32), pltpu.VMEM((tm, LANES), F32)],
        compiler_params=_params(("parallel", "parallel", "arbitrary")),
        name="branches",
    )(outs[0], outs[1], outs[2], lses[0], lses[1], lses[2], proj, y_four, proj,
      w_attn, w_four, proj, proj, bias, bias)


def _outproj_kernel(a_ref, w_ref, x_ref, o_ref):
    o_ref[...] = x_ref[...] + jnp.dot(a_ref[...], w_ref[...], preferred_element_type=F32)


def _outproj(a, w, x, tm=512, tn=1024):
    m, k = a.shape
    n = w.shape[1]
    return pl.pallas_call(
        _outproj_kernel,
        grid=(m // tm, n // tn),
        in_specs=[pl.BlockSpec((tm, k), lambda i, j: (i, 0)),
                  pl.BlockSpec((k, tn), lambda i, j: (0, j)),
                  pl.BlockSpec((tm, tn), lambda i, j: (i, j))],
        out_specs=pl.BlockSpec((tm, tn), lambda i, j: (i, j)),
        out_shape=jax.ShapeDtypeStruct((m, n), F32),
        compiler_params=_params(("parallel", "arbitrary")),
        name="outproj",
    )(a, w, x)


def _rotary_tables(seq, dilation):
    half = HEAD_DIM // 2
    inv_freq = ROPE_THETA ** (-jnp.arange(half, dtype=F32) * (2.0 / HEAD_DIM))
    ang = jnp.arange(seq, dtype=jnp.int32).astype(F32)[:, None] * inv_freq[None, :]
    cos, sin = jnp.cos(ang), jnp.sin(ang)
    cos_f = jnp.concatenate([cos, cos], axis=-1)
    sin_f = jnp.concatenate([-sin, sin], axis=-1)
    scale = HEAD_DIM ** -0.5
    tabs = []
    for t in (cos_f, sin_f):
        t = t.reshape(seq // dilation, dilation, HEAD_DIM).transpose(1, 0, 2).reshape(seq, HEAD_DIM)
        tabs.append(jnp.stack([t * scale, t]))
    return tabs


def kernel(x, norm_gain, w_in, gate_bias, w_branch_attn, w_branch_fourier, w_out, final_norm_gain):
    batch, seq, d = x.shape
    depth = norm_gain.shape[0]
    attn_width = w_branch_attn.shape[1]
    four_width = w_branch_fourier.shape[1]
    n_heads = attn_width // HEAD_DIM
    tn = attn_width
    assert four_width == 2 * tn and d % tn == 0
    assert w_in.shape[2] == (3 * N_GROUPS + 1) * tn + 2 * four_width + 2 * d
    assert all(w // (2 * dil) == ATTN_RADIUS for w, dil in DILATED_GROUPS)
    dilations = tuple(dil for _, dil in DILATED_GROUPS)
    assert dilations[0] == 1 and FFT_P in dilations
    q0, k0, v0 = 0, N_GROUPS, 2 * N_GROUPS
    za = 3 * N_GROUPS
    u0 = za + 1
    zf0 = u0 + 2
    g0 = zf0 + 2
    n_gate_tiles = 2 * d // tn

    for layer in range(depth):
        hs = _rmsnorm(x, norm_gain[layer], BF16, dilations[1:])
        w = w_in[layer].astype(BF16)
        outs, lses = [], []
        y_four = None
        proj = None
        for g, dil in enumerate(dilations):
            cos_tab, sin_tab = _rotary_tables(seq, dil)
            tiles = [q0 + g, k0 + g, v0 + g]
            if dil == 1:
                tiles += [za, zf0, zf0 + 1] + [g0 + t for t in range(n_gate_tiles)]
            q_tile = 0
            if dil == FFT_P:
                tiles = [u0, u0 + 1] + tiles
                q_tile = 2
            pg = _inproj(hs[g].reshape(batch * seq, d), w, tiles, q_tile, cos_tab, sin_tab, seq, tn=tn)
            cols = pg.shape[1]
            o_g, lse_g = _banded_attention(pg.reshape(batch * dil, seq // dil, cols), q_tile, n_heads)
            if dil == 1:
                proj = pg.reshape(batch, seq, cols)
                outs.append(o_g)
                lses.append(lse_g)
            else:
                outs.append(o_g.reshape(batch, dil, seq // dil, attn_width))
                lses.append(lse_g.reshape(batch, dil, seq // dil, LANES))
            if dil == FFT_P:
                y_four = _fourier_mix(pg.reshape(batch, dil, seq // dil, cols), 0, batch, seq, four_width)
        mixed = _branches(proj, outs, lses, y_four, w_branch_attn[layer].astype(BF16),
                          w_branch_fourier[layer].astype(BF16), gate_bias[layer],
                          3, 2, 6, n_heads)
        x = _outproj(mixed.reshape(batch * seq, d), w_out[layer].astype(BF16),
                     x.reshape(batch * seq, d)).reshape(batch, seq, d)
    return _rmsnorm(x, final_norm_gain, x.dtype)[0]
```

```python
import functools

import numpy as np
import jax
import jax.numpy as jnp
from jax import lax
from jax.experimental import pallas as pl
from jax.experimental.pallas import tpu as pltpu

HEAD_DIM = 128
DILATED_GROUPS = ((128, 1), (512, 4), (2048, 16))
N_GROUPS = len(DILATED_GROUPS)
FOURIER_GROUPS = 8
ROPE_THETA = 10000.0
NORM_EPS = 1e-6
NEG_INF = -1e30
ATTN_RADIUS = 64
FFT_P = 16
FFT_F2_BLOCK = 16
LANES = 128
VMEM_LIMIT = 56 * 1024 * 1024

F32 = jnp.float32
BF16 = jnp.bfloat16


def _params(sem, vmem=VMEM_LIMIT):
    return pltpu.CompilerParams(dimension_semantics=sem, vmem_limit_bytes=vmem)


def _rmsnorm_kernel(x_ref, g_ref, *refs, dilations):
    o_ref = refs[0]
    x = x_ref[...].astype(F32)
    ms = jnp.mean(x * x, axis=-1, keepdims=True)
    y = x * lax.rsqrt(ms + NORM_EPS) * g_ref[...]
    o_ref[...] = y.astype(o_ref.dtype)
    if dilations:
        ybuf = refs[-1]
        rows = y.shape[0]
        chunks = [slice(c * LANES, (c + 1) * LANES) for c in range(y.shape[1] // LANES)]
        for c, sl in enumerate(chunks):
            ybuf[c] = y[:, sl]
        for d, od_ref in zip(dilations, refs[1:-1]):
            for r in range(d):
                for c, sl in enumerate(chunks):
                    od_ref[r, :, sl] = ybuf[c, pl.ds(r, rows // d, stride=d), :].astype(od_ref.dtype)


def _rmsnorm(x, gain, out_dtype, dilations=(), tm=512):
    batch, seq, d = x.shape
    out_specs = [pl.BlockSpec((None, tm, d), lambda b, i: (b, i, 0))]
    out_shape = [jax.ShapeDtypeStruct((batch, seq, d), out_dtype)]
    for dil in dilations:
        out_specs.append(pl.BlockSpec((None, dil, tm // dil, d), lambda b, i: (b, 0, i, 0)))
        out_shape.append(jax.ShapeDtypeStruct((batch, dil, seq // dil, d), out_dtype))
    return pl.pallas_call(
        functools.partial(_rmsnorm_kernel, dilations=tuple(dilations)),
        grid=(batch, seq // tm),
        in_specs=[pl.BlockSpec((None, tm, d), lambda b, i: (b, i, 0)),
                  pl.BlockSpec((1, d), lambda b, i: (0, 0))],
        out_specs=out_specs,
        out_shape=out_shape,
        scratch_shapes=[pltpu.VMEM((d // LANES, tm, LANES), F32)] if dilations else [],
        compiler_params=_params(("parallel", "parallel")),
        name="rmsnorm",
    )(x, gain.reshape(1, d).astype(F32))


def _inproj_kernel(tiles_ref, h_ref, w_ref, cos0_ref, cos1_ref, sin0_ref, sin1_ref, o_ref, *, rot_lo):
    del tiles_ref
    j = pl.program_id(1)
    acc = jnp.dot(h_ref[...], w_ref[...], preferred_element_type=F32)
    is_rot = (j >= rot_lo) & (j < rot_lo + 2)

    @pl.when(is_rot)
    def _():
        scale = jnp.where(j == rot_lo, HEAD_DIM ** -0.5, 1.0).astype(F32)
        half = acc.shape[0] // 2
        for a, (cos_ref, sin_ref) in enumerate(((cos0_ref, sin0_ref), (cos1_ref, sin1_ref))):
            rows = slice(a * half, (a + 1) * half)
            c = cos_ref[...] * scale
            s = sin_ref[...] * scale
            for hd in range(acc.shape[1] // HEAD_DIM):
                sl = slice(hd * HEAD_DIM, (hd + 1) * HEAD_DIM)
                xh = acc[rows, sl]
                o_ref[rows, sl] = (xh * c + pltpu.roll(xh, HEAD_DIM // 2, axis=1) * s).astype(o_ref.dtype)

    @pl.when(jnp.logical_not(is_rot))
    def _():
        o_ref[...] = acc.astype(o_ref.dtype)


def _inproj(h, w, col_tiles, rot_lo, cos_tab, sin_tab, seq, dilation, tm=1024, tn=1024):
    m, k = h.shape
    sub = seq // dilation
    half = tm // 2
    assert sub % half == 0
    cos_v = cos_tab.reshape(sub, dilation * HEAD_DIM)
    sin_v = sin_tab.reshape(sub, dilation * HEAD_DIM)

    def tab_spec(a):
        def index(i, j, t):
            pos = (i * tm + a * half) % seq
            return ((pos % sub) // half, pos // sub)
        return pl.BlockSpec((half, HEAD_DIM), index)

    grid_spec = pltpu.PrefetchScalarGridSpec(
        num_scalar_prefetch=1,
        grid=(m // tm, len(col_tiles)),
        in_specs=[pl.BlockSpec((tm, k), lambda i, j, t: (i, 0)),
                  pl.BlockSpec((k, tn), lambda i, j, t: (0, t[j])),
                  tab_spec(0), tab_spec(1), tab_spec(0), tab_spec(1)],
        out_specs=pl.BlockSpec((tm, tn), lambda i, j, t: (i, j)),
    )
    return pl.pallas_call(
        functools.partial(_inproj_kernel, rot_lo=rot_lo),
        grid_spec=grid_spec,
        out_shape=jax.ShapeDtypeStruct((m, len(col_tiles) * tn), BF16),
        compiler_params=_params(("parallel", "arbitrary")),
        name="inproj",
    )(jnp.asarray(np.asarray(col_tiles, np.int32)), h, w, cos_v, cos_v, sin_v, sin_v)


def _attn_kernel(q_ref, kp_ref, kc_ref, kn_ref, vp_ref, vc_ref, vn_ref, o_ref, lse_ref,
                 kbuf, vbuf, *, tq, sub, n_heads):
    it = pl.program_id(1)
    r = ATTN_RADIUS
    qb = 2 * r
    kb = 4 * r
    kbuf[0:r, :] = kp_ref[...]
    kbuf[r:r + tq, :] = kc_ref[...]
    kbuf[r + tq:, :] = kn_ref[...]
    vbuf[0:r, :] = vp_ref[...]
    vbuf[r:r + tq, :] = vc_ref[...]
    vbuf[r + tq:, :] = vn_ref[...]

    def body(i, carry):
        r0 = pl.multiple_of(i * qb, qb)
        row = lax.broadcasted_iota(jnp.int32, (qb, kb), 0)
        col = lax.broadcasted_iota(jnp.int32, (qb, kb), 1)
        delta = col - row
        kpos = col + (it * tq + i * qb - r)
        valid = (delta >= 0) & (delta <= 2 * r) & (kpos >= 0) & (kpos < sub)
        lane = lax.broadcasted_iota(jnp.int32, (qb, LANES), 1)
        lse_tile = jnp.zeros((qb, LANES), F32)
        for h in range(n_heads):
            sl = slice(h * HEAD_DIM, (h + 1) * HEAD_DIM)
            q = q_ref[pl.ds(r0, qb), sl]
            k = kbuf[pl.ds(r0, kb), sl]
            v = vbuf[pl.ds(r0, kb), sl]
            s = lax.dot_general(q, k, (((1,), (1,)), ((), ())), preferred_element_type=F32)
            s = jnp.where(valid, s, NEG_INF)
            m = jnp.max(s, axis=-1, keepdims=True)
            p = jnp.exp(s - m)
            l = jnp.sum(p, axis=-1, keepdims=True)
            o = jnp.dot(p.astype(BF16), v, preferred_element_type=F32)
            o_ref[pl.ds(r0, qb), sl] = (o / l).astype(o_ref.dtype)
            lse_tile = jnp.where(lane == h, m + jnp.log(l), lse_tile)
        lse_ref[pl.ds(r0, qb), :] = lse_tile
        return carry

    lax.fori_loop(0, tq // qb, body, 0)


def _banded_attention(qkv, q_tile, n_heads, tq=512):
    n, sub, _ = qkv.shape
    width = n_heads * HEAD_DIM
    tq = min(tq, sub)
    r = ATTN_RADIUS
    per = tq // r

    def cur(off):
        return pl.BlockSpec((None, tq, width), lambda b, t: (b, t, q_tile + off))

    def prev(off):
        return pl.BlockSpec((None, r, width), lambda b, t: (b, jnp.maximum(t * per - 1, 0), q_tile + off))

    def nxt(off):
        return pl.BlockSpec((None, r, width),
                            lambda b, t: (b, jnp.minimum((t + 1) * per, sub // r - 1), q_tile + off))

    return pl.pallas_call(
        functools.partial(_attn_kernel, tq=tq, sub=sub, n_heads=n_heads),
        grid=(n, sub // tq),
        in_specs=[cur(0), prev(1), cur(1), nxt(1), prev(2), cur(2), nxt(2)],
        out_specs=[pl.BlockSpec((None, tq, width), lambda b, t: (b, t, 0)),
                   pl.BlockSpec((None, tq, LANES), lambda b, t: (b, t, 0))],
        out_shape=[jax.ShapeDtypeStruct((n, sub, width), BF16),
                   jax.ShapeDtypeStruct((n, sub, LANES), F32)],
        scratch_shapes=[pltpu.VMEM((tq + 2 * r, width), BF16),
                        pltpu.VMEM((tq + 2 * r, width), BF16)],
        compiler_params=_params(("parallel", "parallel")),
        name=f"attn_sub{sub}",
    )(qkv, qkv, qkv, qkv, qkv, qkv, qkv)


def _dft_tables(seq, gdim):
    p, fb = FFT_P, FFT_F2_BLOCK
    q = seq // p
    c = np.arange(gdim, dtype=np.float64)
    ang = 2 * np.pi * np.outer(c, c) / gdim
    wc = np.concatenate([np.cos(ang), np.sin(ang)], axis=1) / np.sqrt(gdim)
    aq = 2 * np.pi * (np.outer(np.arange(q), np.arange(q)) % q) / q
    cq, sq = np.cos(aq) / np.sqrt(q), np.sin(aq) / np.sqrt(q)
    w1 = np.block([[cq, -sq], [-sq, -cq]])
    f = (q * np.arange(p)[None, :, None, None]
         + fb * np.arange(q // fb)[:, None, None, None] + np.arange(fb)[None, None, :, None])
    phi = 2 * np.pi * ((f * np.arange(p)[None, None, None, :]) % seq) / seq
    eye = np.eye(fb)
    parts = [np.einsum('afjs,jk->afjsk', t / np.sqrt(p), eye) for t in (np.cos(phi), np.sin(phi))]
    big = np.stack(parts, axis=3).reshape(q // fb, p * fb, 2 * p * fb)
    as_bf16 = lambda a: jnp.asarray(a.astype(np.float32)).astype(BF16)
    return as_bf16(wc), as_bf16(w1), as_bf16(big)


def _fft1_kernel(u_ref, wc_ref, w1_ref, h_ref, *, gdim):
    q = u_ref.shape[0]
    for g in range(u_ref.shape[1] // gdim):
        sl = slice(g * gdim, (g + 1) * gdim)
        ab = jnp.dot(u_ref[:, sl], wc_ref[...], preferred_element_type=F32)
        stack = jnp.concatenate([ab[:, :gdim], ab[:, gdim:]], axis=0).astype(BF16)
        hh = jnp.dot(w1_ref[...], stack, preferred_element_type=F32)
        h_ref[0, :, sl] = hh[:q].astype(h_ref.dtype)
        h_ref[1, :, sl] = hh[q:].astype(h_ref.dtype)


def _sigmoid(z):
    return 0.5 * jnp.tanh(0.5 * z) + 0.5


def _silu(z):
    return z * _sigmoid(z)


def _fft2_kernel(h_ref, big_ref, z_ref, o_ref):
    parts, p, fb, width = h_ref.shape
    rhs = h_ref[...].reshape(parts * p * fb, width)
    y = jnp.dot(big_ref[...], rhs, preferred_element_type=F32)
    o_ref[...] = (y.reshape(p, fb, width) * _silu(z_ref[...].astype(F32))).astype(o_ref.dtype)


def _fourier_mix(u_cm, u_tile, proj, z_tile, batch, seq, width):
    gdim = width // FOURIER_GROUPS
    p, fb = FFT_P, FFT_F2_BLOCK
    q = seq // p
    wc, w1, big = _dft_tables(seq, gdim)
    h = pl.pallas_call(
        functools.partial(_fft1_kernel, gdim=gdim),
        grid=(batch, p),
        in_specs=[pl.BlockSpec((None, None, q, width), lambda b, s: (b, s, 0, u_tile)),
                  pl.BlockSpec(wc.shape, lambda b, s: (0, 0)),
                  pl.BlockSpec(w1.shape, lambda b, s: (0, 0))],
        out_specs=pl.BlockSpec((None, 2, None, q, width), lambda b, s: (b, 0, s, 0, 0)),
        out_shape=jax.ShapeDtypeStruct((batch, 2, p, q, width), BF16),
        compiler_params=_params(("parallel", "parallel")),
        name="fft_seq1",
    )(u_cm, wc, w1)
    y = pl.pallas_call(
        _fft2_kernel,
        grid=(batch, q // fb),
        in_specs=[pl.BlockSpec((None, 2, p, fb, width), lambda b, a: (b, 0, 0, a, 0)),
                  pl.BlockSpec((None, p * fb, 2 * p * fb), lambda b, a: (a, 0, 0)),
                  pl.BlockSpec((None, p, None, fb, width), lambda b, a: (b, 0, a, 0, z_tile))],
        out_specs=pl.BlockSpec((None, p, None, fb, width), lambda b, a: (b, 0, a, 0, 0)),
        out_shape=jax.ShapeDtypeStruct((batch, p, q // fb, fb, width), BF16),
        compiler_params=_params(("parallel", "parallel")),
        name="fft_seq2",
    )(h, big, proj.reshape(batch, p, q // fb, fb, proj.shape[-1]))
    return y.reshape(batch, seq, width)


def _branch_kernel(o1_ref, o2_ref, o3_ref, l1_ref, l2_ref, l3_ref, za_ref, af_ref,
                   wa_ref, wf_ref, ga_ref, gf_ref, ba_ref, bf_ref, out_ref,
                   a_attn, on2, on3, ln2, ln3, *, n_heads):
    @pl.when(pl.program_id(2) == 0)
    def _():
        rows = ln2.shape[0]
        for src, dst in ((o2_ref, on2), (o3_ref, on3)):
            d = src.shape[0]
            for r in range(d):
                for h in range(n_heads):
                    dst[h, pl.ds(r, rows // d, stride=d), :] = (
                        src[r, :, h * HEAD_DIM:(h + 1) * HEAD_DIM].astype(F32))
        for src, dst in ((l2_ref, ln2), (l3_ref, ln3)):
            d = src.shape[0]
            for r in range(d):
                dst[pl.ds(r, rows // d, stride=d), :] = src[r]
        l1, l2, l3 = l1_ref[...], ln2[...], ln3[...]
        m = jnp.maximum(jnp.maximum(l1, l2), l3)
        e1, e2, e3 = jnp.exp(l1 - m), jnp.exp(l2 - m), jnp.exp(l3 - m)
        inv = 1.0 / (e1 + e2 + e3)
        al1, al2, al3 = e1 * inv, e2 * inv, e3 * inv
        for h in range(n_heads):
            sl = slice(h * HEAD_DIM, (h + 1) * HEAD_DIM)
            o = (al1[:, h:h + 1] * o1_ref[:, sl].astype(F32)
                 + al2[:, h:h + 1] * on2[h]
                 + al3[:, h:h + 1] * on3[h])
            a_attn[:, sl] = (o * _silu(za_ref[:, sl].astype(F32))).astype(a_attn.dtype)

    br_a = jnp.dot(a_attn[...], wa_ref[...], preferred_element_type=F32)
    br_f = jnp.dot(af_ref[...], wf_ref[...], preferred_element_type=F32)
    g_a = _sigmoid(ga_ref[...].astype(F32) + ba_ref[...])
    g_f = _sigmoid(gf_ref[...].astype(F32) + bf_ref[...])
    out_ref[...] = (g_a * br_a + g_f * br_f).astype(out_ref.dtype)


def _branches(proj, outs, lses, a_four, w_attn, w_four, gate_bias, za_tile, gate_tile,
              n_heads, tm=512, tn=1024):
    batch, seq, _ = proj.shape
    aw, fw = w_attn.shape[0], w_four.shape[0]
    d = w_attn.shape[1]
    nj = d // tn
    bias = gate_bias.reshape(1, 2 * d).astype(F32)

    def nat(width, tile=0):
        return pl.BlockSpec((None, tm, width), lambda b, i, j: (b, i, tile))

    def cm(arr):
        dil = arr.shape[1]
        return pl.BlockSpec((None, dil, tm // dil, arr.shape[3]), lambda b, i, j: (b, 0, i, 0))

    return pl.pallas_call(
        functools.partial(_branch_kernel, n_heads=n_heads),
        grid=(batch, seq // tm, nj),
        in_specs=[nat(aw), cm(outs[1]), cm(outs[2]), nat(LANES), cm(lses[1]), cm(lses[2]),
                  nat(aw, za_tile), nat(fw),
                  pl.BlockSpec((aw, tn), lambda b, i, j: (0, j)),
                  pl.BlockSpec((fw, tn), lambda b, i, j: (0, j)),
                  pl.BlockSpec((None, tm, tn), lambda b, i, j: (b, i, gate_tile + j)),
                  pl.BlockSpec((None, tm, tn), lambda b, i, j: (b, i, gate_tile + nj + j)),
                  pl.BlockSpec((1, tn), lambda b, i, j: (0, j)),
                  pl.BlockSpec((1, tn), lambda b, i, j: (0, nj + j))],
        out_specs=pl.BlockSpec((None, tm, tn), lambda b, i, j: (b, i, j)),
        out_shape=jax.ShapeDtypeStruct((batch, seq, d), BF16),
        scratch_shapes=[pltpu.VMEM((tm, aw), BF16),
                        pltpu.VMEM((n_heads, tm, HEAD_DIM), F32), pltpu.VMEM((n_heads, tm, HEAD_DIM), F32),
                        pltpu.VMEM((tm, LANES), F32), pltpu.VMEM((tm, LANES), F32)],
        compiler_params=_params(("parallel", "parallel", "arbitrary")),
        name="branches",
    )(outs[0], outs[1], outs[2], lses[0], lses[1], lses[2], proj, a_four,
      w_attn, w_four, proj, proj, bias, bias)


def _outproj_kernel(a_ref, w_ref, x_ref, *refs, nj, final_norm):
    o_ref = refs[-1]
    j = pl.program_id(1)
    tn = w_ref.shape[1]
    y = x_ref[...] + jnp.dot(a_ref[...], w_ref[...], preferred_element_type=F32)
    for jj in range(nj):
        @pl.when(j == jj)
        def _(jj=jj):
            o_ref[:, jj * tn:(jj + 1) * tn] = y

    if final_norm:
        @pl.when(j == nj - 1)
        def _():
            chunk = 128
            for c in range(o_ref.shape[0] // chunk):
                rows = slice(c * chunk, (c + 1) * chunk)
                v = o_ref[rows, :]
                ms = jnp.mean(v * v, axis=-1, keepdims=True)
                o_ref[rows, :] = v * lax.rsqrt(ms + NORM_EPS) * refs[0][...]


def _outproj(a, w, x, final_gain=None, tm=512, tn=1024):
    m, k = a.shape
    n = w.shape[1]
    nj = n // tn
    operands = [a, w, x]
    in_specs = [pl.BlockSpec((tm, k), lambda i, j: (i, 0)),
                pl.BlockSpec((k, tn), lambda i, j: (0, j)),
                pl.BlockSpec((tm, tn), lambda i, j: (i, j))]
    if final_gain is not None:
        operands.append(final_gain.reshape(1, n).astype(F32))
        in_specs.append(pl.BlockSpec((1, n), lambda i, j: (0, 0)))
    return pl.pallas_call(
        functools.partial(_outproj_kernel, nj=nj, final_norm=final_gain is not None),
        grid=(m // tm, nj),
        in_specs=in_specs,
        out_specs=pl.BlockSpec((tm, n), lambda i, j: (i, 0)),
        out_shape=jax.ShapeDtypeStruct((m, n), F32),
        compiler_params=_params(("parallel", "arbitrary")),
        name="outproj",
    )(*operands)


def _rotary_tables(seq):
    half = HEAD_DIM // 2
    inv_freq = ROPE_THETA ** (-jnp.arange(half, dtype=F32) * (2.0 / HEAD_DIM))
    ang = jnp.arange(seq, dtype=jnp.int32).astype(F32)[:, None] * inv_freq[None, :]
    cos, sin = jnp.cos(ang), jnp.sin(ang)
    return jnp.concatenate([cos, cos], axis=-1), jnp.concatenate([-sin, sin], axis=-1)


def kernel(x, norm_gain, w_in, gate_bias, w_branch_attn, w_branch_fourier, w_out, final_norm_gain):
    batch, seq, d = x.shape
    depth = norm_gain.shape[0]
    attn_width = w_branch_attn.shape[1]
    four_width = w_branch_fourier.shape[1]
    n_heads = attn_width // HEAD_DIM
    tn = attn_width
    assert four_width == 2 * tn and d % tn == 0
    assert w_in.shape[2] == (3 * N_GROUPS + 1) * tn + 2 * four_width + 2 * d
    assert all(w // (2 * dil) == ATTN_RADIUS for w, dil in DILATED_GROUPS)
    dilations = tuple(dil for _, dil in DILATED_GROUPS)
    assert dilations[0] == 1 and FFT_P in dilations
    q0, k0, v0 = 0, N_GROUPS, 2 * N_GROUPS
    za = 3 * N_GROUPS
    u0 = za + 1
    zf0 = u0 + 2
    g0 = zf0 + 2
    n_gate_tiles = 2 * d // tn

    cos_tab, sin_tab = _rotary_tables(seq)
    for layer in range(depth):
        hs = _rmsnorm(x, norm_gain[layer], BF16, dilations[1:])
        w = w_in[layer].astype(BF16)
        outs, lses = [], []
        a_four = None
        proj = None
        for g, dil in enumerate(dilations):
            tiles = [q0 + g, k0 + g, v0 + g]
            if dil == 1:
                tiles += [za, zf0, zf0 + 1] + [g0 + t for t in range(n_gate_tiles)]
            q_tile = 0
            if dil == FFT_P:
                tiles = [u0, u0 + 1] + tiles
                q_tile = 2
            pg = _inproj(hs[g].reshape(batch * seq, d), w, tiles, q_tile, cos_tab, sin_tab, seq, dil, tn=tn)
            cols = pg.shape[1]
            o_g, lse_g = _banded_attention(pg.reshape(batch * dil, seq // dil, cols), q_tile, n_heads)
            if dil == 1:
                proj = pg.reshape(batch, seq, cols)
                outs.append(o_g)
                lses.append(lse_g)
            else:
                outs.append(o_g.reshape(batch, dil, seq // dil, attn_width))
                lses.append(lse_g.reshape(batch, dil, seq // dil, LANES))
            if dil == FFT_P:
                a_four = _fourier_mix(pg.reshape(batch, dil, seq // dil, cols), 0, proj, 2,
                                      batch, seq, four_width)
        mixed = _branches(proj, outs, lses, a_four, w_branch_attn[layer].astype(BF16),
                          w_branch_fourier[layer].astype(BF16), gate_bias[layer], 3, 6, n_heads)
        x = _outproj(mixed.reshape(batch * seq, d), w_out[layer].astype(BF16), x.reshape(batch * seq, d),
                     final_norm_gain if layer == depth - 1 else None).reshape(batch, seq, d)
    return x
```

```python
import functools

import numpy as np
import jax
import jax.numpy as jnp
from jax import lax
from jax.experimental import pallas as pl
from jax.experimental.pallas import tpu as pltpu

HEAD_DIM = 128
DILATED_GROUPS = ((128, 1), (512, 4), (2048, 16))
N_GROUPS = len(DILATED_GROUPS)
FOURIER_GROUPS = 8
ROPE_THETA = 10000.0
NORM_EPS = 1e-6
NEG_INF = -1e30
ATTN_RADIUS = 64
FFT_P = 16
FFT_F2_BLOCK = 16
LANES = 128
VMEM_LIMIT = 56 * 1024 * 1024

F32 = jnp.float32
BF16 = jnp.bfloat16


def _params(sem, vmem=VMEM_LIMIT):
    return pltpu.CompilerParams(dimension_semantics=sem, vmem_limit_bytes=vmem)


def _rmsnorm_kernel(x_ref, g_ref, *refs, dilations):
    o_ref = refs[0]
    x = x_ref[...].astype(F32)
    ms = jnp.mean(x * x, axis=-1, keepdims=True)
    y = x * lax.rsqrt(ms + NORM_EPS) * g_ref[...]
    o_ref[...] = y.astype(o_ref.dtype)
    if dilations:
        ybuf = refs[-1]
        rows = y.shape[0]
        chunks = [slice(c * LANES, (c + 1) * LANES) for c in range(y.shape[1] // LANES)]
        for c, sl in enumerate(chunks):
            ybuf[c] = y[:, sl]
        for d, od_ref in zip(dilations, refs[1:-1]):
            for r in range(d):
                for c, sl in enumerate(chunks):
                    od_ref[r, :, sl] = ybuf[c, pl.ds(r, rows // d, stride=d), :].astype(od_ref.dtype)


def _rmsnorm(x, gain, out_dtype, dilations=(), tm=512):
    batch, seq, d = x.shape
    out_specs = [pl.BlockSpec((None, tm, d), lambda b, i: (b, i, 0))]
    out_shape = [jax.ShapeDtypeStruct((batch, seq, d), out_dtype)]
    for dil in dilations:
        out_specs.append(pl.BlockSpec((None, dil, tm // dil, d), lambda b, i: (b, 0, i, 0)))
        out_shape.append(jax.ShapeDtypeStruct((batch, dil, seq // dil, d), out_dtype))
    return pl.pallas_call(
        functools.partial(_rmsnorm_kernel, dilations=tuple(dilations)),
        grid=(batch, seq // tm),
        in_specs=[pl.BlockSpec((None, tm, d), lambda b, i: (b, i, 0)),
                  pl.BlockSpec((1, d), lambda b, i: (0, 0))],
        out_specs=out_specs,
        out_shape=out_shape,
        scratch_shapes=[pltpu.VMEM((d // LANES, tm, LANES), F32)] if dilations else [],
        compiler_params=_params(("parallel", "parallel")),
        name="rmsnorm",
    )(x, gain.reshape(1, d).astype(F32))


def _inproj_kernel(tiles_ref, h_ref, w_ref, cos0_ref, cos1_ref, sin0_ref, sin1_ref, o_ref, w_bf, *, rot_lo):
    del tiles_ref
    j = pl.program_id(0)

    @pl.when(pl.program_id(1) == 0)
    def _():
        chunk = 512
        for c in range(w_ref.shape[0] // chunk):
            rows = slice(c * chunk, (c + 1) * chunk)
            w_bf[rows, :] = w_ref[rows, :].astype(w_bf.dtype)

    acc = jnp.dot(h_ref[...], w_bf[...], preferred_element_type=F32)
    is_rot = (j >= rot_lo) & (j < rot_lo + 2)

    @pl.when(is_rot)
    def _():
        scale = jnp.where(j == rot_lo, HEAD_DIM ** -0.5, 1.0).astype(F32)
        half = acc.shape[0] // 2
        for a, (cos_ref, sin_ref) in enumerate(((cos0_ref, sin0_ref), (cos1_ref, sin1_ref))):
            rows = slice(a * half, (a + 1) * half)
            c = cos_ref[...] * scale
            s = sin_ref[...] * scale
            for hd in range(acc.shape[1] // HEAD_DIM):
                sl = slice(hd * HEAD_DIM, (hd + 1) * HEAD_DIM)
                xh = acc[rows, sl]
                o_ref[rows, sl] = (xh * c + pltpu.roll(xh, HEAD_DIM // 2, axis=1) * s).astype(o_ref.dtype)

    @pl.when(jnp.logical_not(is_rot))
    def _():
        o_ref[...] = acc.astype(o_ref.dtype)


def _inproj(h, w, col_tiles, rot_lo, cos_tab, sin_tab, seq, dilation, tm=512, tn=1024):
    m, k = h.shape
    sub = seq // dilation
    half = tm // 2
    assert sub % half == 0
    cos_v = cos_tab.reshape(sub, dilation * HEAD_DIM)
    sin_v = sin_tab.reshape(sub, dilation * HEAD_DIM)

    def tab_spec(a):
        def index(j, i, t):
            pos = (i * tm + a * half) % seq
            return ((pos % sub) // half, pos // sub)
        return pl.BlockSpec((half, HEAD_DIM), index)

    grid_spec = pltpu.PrefetchScalarGridSpec(
        num_scalar_prefetch=1,
        grid=(len(col_tiles), m // tm),
        in_specs=[pl.BlockSpec((tm, k), lambda j, i, t: (i, 0)),
                  pl.BlockSpec((k, tn), lambda j, i, t: (0, t[j])),
                  tab_spec(0), tab_spec(1), tab_spec(0), tab_spec(1)],
        out_specs=pl.BlockSpec((tm, tn), lambda j, i, t: (i, j)),
        scratch_shapes=[pltpu.VMEM((k, tn), BF16)],
    )
    return pl.pallas_call(
        functools.partial(_inproj_kernel, rot_lo=rot_lo),
        grid_spec=grid_spec,
        out_shape=jax.ShapeDtypeStruct((m, len(col_tiles) * tn), BF16),
        compiler_params=_params(("parallel", "arbitrary")),
        name="inproj",
    )(jnp.asarray(np.asarray(col_tiles, np.int32)), h, w, cos_v, cos_v, sin_v, sin_v)


def _attn_kernel(q_ref, kp_ref, kc_ref, kn_ref, vp_ref, vc_ref, vn_ref, o_ref, lse_ref,
                 kbuf, vbuf, *, tq, sub, n_heads):
    it = pl.program_id(1)
    r = ATTN_RADIUS
    qb = 2 * r
    kb = 4 * r
    kbuf[0:r, :] = kp_ref[...]
    kbuf[r:r + tq, :] = kc_ref[...]
    kbuf[r + tq:, :] = kn_ref[...]
    vbuf[0:r, :] = vp_ref[...]
    vbuf[r:r + tq, :] = vc_ref[...]
    vbuf[r + tq:, :] = vn_ref[...]

    def body(i, carry):
        r0 = pl.multiple_of(i * qb, qb)
        row = lax.broadcasted_iota(jnp.int32, (qb, kb), 0)
        col = lax.broadcasted_iota(jnp.int32, (qb, kb), 1)
        delta = col - row
        kpos = col + (it * tq + i * qb - r)
        valid = (delta >= 0) & (delta <= 2 * r) & (kpos >= 0) & (kpos < sub)
        lane = lax.broadcasted_iota(jnp.int32, (qb, LANES), 1)
        lse_tile = jnp.zeros((qb, LANES), F32)
        for h in range(n_heads):
            sl = slice(h * HEAD_DIM, (h + 1) * HEAD_DIM)
            q = q_ref[pl.ds(r0, qb), sl]
            k = kbuf[pl.ds(r0, kb), sl]
            v = vbuf[pl.ds(r0, kb), sl]
            s = lax.dot_general(q, k, (((1,), (1,)), ((), ())), preferred_element_type=F32)
            s = jnp.where(valid, s, NEG_INF)
            m = jnp.max(s, axis=-1, keepdims=True)
            p = jnp.exp(s - m)
            l = jnp.sum(p, axis=-1, keepdims=True)
            o = jnp.dot(p.astype(BF16), v, preferred_element_type=F32)
            o_ref[pl.ds(r0, qb), sl] = (o / l).astype(o_ref.dtype)
            lse_tile = jnp.where(lane == h, m + jnp.log(l), lse_tile)
        lse_ref[pl.ds(r0, qb), :] = lse_tile
        return carry

    lax.fori_loop(0, tq // qb, body, 0)


def _banded_attention(qkv, q_tile, n_heads, tq=512):
    n, sub, _ = qkv.shape
    width = n_heads * HEAD_DIM
    tq = min(tq, sub)
    r = ATTN_RADIUS
    per = tq // r

    def cur(off):
        return pl.BlockSpec((None, tq, width), lambda b, t: (b, t, q_tile + off))

    def prev(off):
        return pl.BlockSpec((None, r, width), lambda b, t: (b, jnp.maximum(t * per - 1, 0), q_tile + off))

    def nxt(off):
        return pl.BlockSpec((None, r, width),
                            lambda b, t: (b, jnp.minimum((t + 1) * per, sub // r - 1), q_tile + off))

    return pl.pallas_call(
        functools.partial(_attn_kernel, tq=tq, sub=sub, n_heads=n_heads),
        grid=(n, sub // tq),
        in_specs=[cur(0), prev(1), cur(1), nxt(1), prev(2), cur(2), nxt(2)],
        out_specs=[pl.BlockSpec((None, tq, width), lambda b, t: (b, t, 0)),
                   pl.BlockSpec((None, tq, LANES), lambda b, t: (b, t, 0))],
        out_shape=[jax.ShapeDtypeStruct((n, sub, width), BF16),
                   jax.ShapeDtypeStruct((n, sub, LANES), F32)],
        scratch_shapes=[pltpu.VMEM((tq + 2 * r, width), BF16),
                        pltpu.VMEM((tq + 2 * r, width), BF16)],
        compiler_params=_params(("parallel", "parallel")),
        name=f"attn_sub{sub}",
    )(qkv, qkv, qkv, qkv, qkv, qkv, qkv)


def _dft_tables(seq, gdim):
    p, fb = FFT_P, FFT_F2_BLOCK
    q = seq // p
    c = np.arange(gdim, dtype=np.float64)
    ang = 2 * np.pi * np.outer(c, c) / gdim
    wc = np.concatenate([np.cos(ang), np.sin(ang)], axis=1) / np.sqrt(gdim)
    aq = 2 * np.pi * (np.outer(np.arange(q), np.arange(q)) % q) / q
    cq, sq = np.cos(aq) / np.sqrt(q), np.sin(aq) / np.sqrt(q)
    w1 = np.block([[cq, -sq], [-sq, -cq]])
    f = (q * np.arange(p)[None, :, None, None]
         + fb * np.arange(q // fb)[:, None, None, None] + np.arange(fb)[None, None, :, None])
    phi = 2 * np.pi * ((f * np.arange(p)[None, None, None, :]) % seq) / seq
    eye = np.eye(fb)
    parts = [np.einsum('afjs,jk->afjsk', t / np.sqrt(p), eye) for t in (np.cos(phi), np.sin(phi))]
    big = np.stack(parts, axis=3).reshape(q // fb, p * fb, 2 * p * fb)
    as_bf16 = lambda a: jnp.asarray(a.astype(np.float32)).astype(BF16)
    return as_bf16(wc), as_bf16(w1), as_bf16(big)


def _fft1_kernel(u_ref, wc_ref, w1_ref, h_ref, *, gdim):
    q = u_ref.shape[0]
    for g in range(u_ref.shape[1] // gdim):
        sl = slice(g * gdim, (g + 1) * gdim)
        ab = jnp.dot(u_ref[:, sl], wc_ref[...], preferred_element_type=F32)
        stack = jnp.concatenate([ab[:, :gdim], ab[:, gdim:]], axis=0).astype(BF16)
        hh = jnp.dot(w1_ref[...], stack, preferred_element_type=F32)
        h_ref[0, :, sl] = hh[:q].astype(h_ref.dtype)
        h_ref[1, :, sl] = hh[q:].astype(h_ref.dtype)


def _sigmoid(z):
    return 0.5 * jnp.tanh(0.5 * z) + 0.5


def _silu(z):
    return z * _sigmoid(z)


def _fft2_kernel(h_ref, big_ref, z_ref, o_ref):
    parts, p, fb, width = h_ref.shape
    rhs = h_ref[...].reshape(parts * p * fb, width)
    y = jnp.dot(big_ref[...], rhs, preferred_element_type=F32)
    o_ref[...] = (y.reshape(p, fb, width) * _silu(z_ref[...].astype(F32))).astype(o_ref.dtype)


def _fourier_mix(u_cm, u_tile, proj, z_tile, batch, seq, width):
    gdim = width // FOURIER_GROUPS
    p, fb = FFT_P, FFT_F2_BLOCK
    q = seq // p
    wc, w1, big = _dft_tables(seq, gdim)
    h = pl.pallas_call(
        functools.partial(_fft1_kernel, gdim=gdim),
        grid=(batch, p),
        in_specs=[pl.BlockSpec((None, None, q, width), lambda b, s: (b, s, 0, u_tile)),
                  pl.BlockSpec(wc.shape, lambda b, s: (0, 0)),
                  pl.BlockSpec(w1.shape, lambda b, s: (0, 0))],
        out_specs=pl.BlockSpec((None, 2, None, q, width), lambda b, s: (b, 0, s, 0, 0)),
        out_shape=jax.ShapeDtypeStruct((batch, 2, p, q, width), BF16),
        compiler_params=_params(("parallel", "parallel")),
        name="fft_seq1",
    )(u_cm, wc, w1)
    y = pl.pallas_call(
        _fft2_kernel,
        grid=(batch, q // fb),
        in_specs=[pl.BlockSpec((None, 2, p, fb, width), lambda b, a: (b, 0, 0, a, 0)),
                  pl.BlockSpec((None, p * fb, 2 * p * fb), lambda b, a: (a, 0, 0)),
                  pl.BlockSpec((None, p, None, fb, width), lambda b, a: (b, 0, a, 0, z_tile))],
        out_specs=pl.BlockSpec((None, p, None, fb, width), lambda b, a: (b, 0, a, 0, 0)),
        out_shape=jax.ShapeDtypeStruct((batch, p, q // fb, fb, width), BF16),
        compiler_params=_params(("parallel", "parallel")),
        name="fft_seq2",
    )(h, big, proj.reshape(batch, p, q // fb, fb, proj.shape[-1]))
    return y.reshape(batch, seq, width)


def _branch_kernel(o1_ref, o2_ref, o3_ref, l1_ref, l2_ref, l3_ref, za_ref, af_ref,
                   wa_ref, wf_ref, ga_ref, gf_ref, ba_ref, bf_ref, out_ref,
                   a_attn, on2, on3, ln2, ln3, *, n_heads):
    @pl.when(pl.program_id(2) == 0)
    def _():
        rows = ln2.shape[0]
        for src, dst in ((o2_ref, on2), (o3_ref, on3)):
            d = src.shape[0]
            for r in range(d):
                for h in range(n_heads):
                    dst[h, pl.ds(r, rows // d, stride=d), :] = (
                        src[r, :, h * HEAD_DIM:(h + 1) * HEAD_DIM].astype(F32))
        for src, dst in ((l2_ref, ln2), (l3_ref, ln3)):
            d = src.shape[0]
            for r in range(d):
                dst[pl.ds(r, rows // d, stride=d), :] = src[r]
        l1, l2, l3 = l1_ref[...], ln2[...], ln3[...]
        m = jnp.maximum(jnp.maximum(l1, l2), l3)
        e1, e2, e3 = jnp.exp(l1 - m), jnp.exp(l2 - m), jnp.exp(l3 - m)
        inv = 1.0 / (e1 + e2 + e3)
        al1, al2, al3 = e1 * inv, e2 * inv, e3 * inv
        for h in range(n_heads):
            sl = slice(h * HEAD_DIM, (h + 1) * HEAD_DIM)
            o = (al1[:, h:h + 1] * o1_ref[:, sl].astype(F32)
                 + al2[:, h:h + 1] * on2[h]
                 + al3[:, h:h + 1] * on3[h])
            a_attn[:, sl] = (o * _silu(za_ref[:, sl].astype(F32))).astype(a_attn.dtype)

    br_a = jnp.dot(a_attn[...], wa_ref[...], preferred_element_type=F32)
    br_f = jnp.dot(af_ref[...], wf_ref[...], preferred_element_type=F32)
    g_a = _sigmoid(ga_ref[...].astype(F32) + ba_ref[...])
    g_f = _sigmoid(gf_ref[...].astype(F32) + bf_ref[...])
    out_ref[...] = (g_a * br_a + g_f * br_f).astype(out_ref.dtype)


def _branches(proj, outs, lses, a_four, w_attn, w_four, gate_bias, za_tile, gate_off,
              n_heads, tm=1024, tn=512):
    batch, seq, _ = proj.shape
    aw, fw = w_attn.shape[0], w_four.shape[0]
    d = w_attn.shape[1]
    assert gate_off % tn == 0
    gate_tile = gate_off // tn
    nj = d // tn
    bias = gate_bias.reshape(1, 2 * d).astype(F32)

    def nat(width, tile=0):
        return pl.BlockSpec((None, tm, width), lambda b, i, j: (b, i, tile))

    def cm(arr):
        dil = arr.shape[1]
        return pl.BlockSpec((None, dil, tm // dil, arr.shape[3]), lambda b, i, j: (b, 0, i, 0))

    return pl.pallas_call(
        functools.partial(_branch_kernel, n_heads=n_heads),
        grid=(batch, seq // tm, nj),
        in_specs=[nat(aw), cm(outs[1]), cm(outs[2]), nat(LANES), cm(lses[1]), cm(lses[2]),
                  nat(aw, za_tile), nat(fw),
                  pl.BlockSpec((aw, tn), lambda b, i, j: (0, j)),
                  pl.BlockSpec((fw, tn), lambda b, i, j: (0, j)),
                  pl.BlockSpec((None, tm, tn), lambda b, i, j: (b, i, gate_tile + j)),
                  pl.BlockSpec((None, tm, tn), lambda b, i, j: (b, i, gate_tile + nj + j)),
                  pl.BlockSpec((1, tn), lambda b, i, j: (0, j)),
                  pl.BlockSpec((1, tn), lambda b, i, j: (0, nj + j))],
        out_specs=pl.BlockSpec((None, tm, tn), lambda b, i, j: (b, i, j)),
        out_shape=jax.ShapeDtypeStruct((batch, seq, d), BF16),
        scratch_shapes=[pltpu.VMEM((tm, aw), BF16),
                        pltpu.VMEM((n_heads, tm, HEAD_DIM), F32), pltpu.VMEM((n_heads, tm, HEAD_DIM), F32),
                        pltpu.VMEM((tm, LANES), F32), pltpu.VMEM((tm, LANES), F32)],
        compiler_params=_params(("parallel", "parallel", "arbitrary")),
        name="branches",
    )(outs[0], outs[1], outs[2], lses[0], lses[1], lses[2], proj, a_four,
      w_attn, w_four, proj, proj, bias, bias)


def _outproj_kernel(a_ref, w_ref, x_ref, *refs, nj, final_norm):
    o_ref = refs[-1]
    j = pl.program_id(1)
    tn = w_ref.shape[1]
    y = x_ref[...] + jnp.dot(a_ref[...], w_ref[...], preferred_element_type=F32)
    for jj in range(nj):
        @pl.when(j == jj)
        def _(jj=jj):
            o_ref[:, jj * tn:(jj + 1) * tn] = y

    if final_norm:
        @pl.when(j == nj - 1)
        def _():
            chunk = 128
            for c in range(o_ref.shape[0] // chunk):
                rows = slice(c * chunk, (c + 1) * chunk)
                v = o_ref[rows, :]
                ms = jnp.mean(v * v, axis=-1, keepdims=True)
                o_ref[rows, :] = v * lax.rsqrt(ms + NORM_EPS) * refs[0][...]


def _outproj(a, w, x, final_gain=None, tm=512, tn=1024):
    m, k = a.shape
    n = w.shape[1]
    nj = n // tn
    operands = [a, w, x]
    in_specs = [pl.BlockSpec((tm, k), lambda i, j: (i, 0)),
                pl.BlockSpec((k, tn), lambda i, j: (0, j)),
                pl.BlockSpec((tm, tn), lambda i, j: (i, j))]
    if final_gain is not None:
        operands.append(final_gain.reshape(1, n).astype(F32))
        in_specs.append(pl.BlockSpec((1, n), lambda i, j: (0, 0)))
    return pl.pallas_call(
        functools.partial(_outproj_kernel, nj=nj, final_norm=final_gain is not None),
        grid=(m // tm, nj),
        in_specs=in_specs,
        out_specs=pl.BlockSpec((tm, n), lambda i, j: (i, 0)),
        out_shape=jax.ShapeDtypeStruct((m, n), F32),
        compiler_params=_params(("parallel", "arbitrary")),
        name="outproj",
    )(*operands)


def _rotary_tables(seq):
    half = HEAD_DIM // 2
    inv_freq = ROPE_THETA ** (-jnp.arange(half, dtype=F32) * (2.0 / HEAD_DIM))
    ang = jnp.arange(seq, dtype=jnp.int32).astype(F32)[:, None] * inv_freq[None, :]
    cos, sin = jnp.cos(ang), jnp.sin(ang)
    return jnp.concatenate([cos, cos], axis=-1), jnp.concatenate([-sin, sin], axis=-1)


def kernel(x, norm_gain, w_in, gate_bias, w_branch_attn, w_branch_fourier, w_out, final_norm_gain):
    batch, seq, d = x.shape
    depth = norm_gain.shape[0]
    attn_width = w_branch_attn.shape[1]
    four_width = w_branch_fourier.shape[1]
    n_heads = attn_width // HEAD_DIM
    tn = attn_width
    assert four_width == 2 * tn and d % tn == 0
    assert w_in.shape[2] == (3 * N_GROUPS + 1) * tn + 2 * four_width + 2 * d
    assert all(w // (2 * dil) == ATTN_RADIUS for w, dil in DILATED_GROUPS)
    dilations = tuple(dil for _, dil in DILATED_GROUPS)
    assert dilations[0] == 1 and FFT_P in dilations
    q0, k0, v0 = 0, N_GROUPS, 2 * N_GROUPS
    za = 3 * N_GROUPS
    u0 = za + 1
    zf0 = u0 + 2
    g0 = zf0 + 2
    n_gate_tiles = 2 * d // tn

    cos_tab, sin_tab = _rotary_tables(seq)
    for layer in range(depth):
        hs = _rmsnorm(x, norm_gain[layer], BF16, dilations[1:])
        w = w_in[layer]
        outs, lses = [], []
        a_four = None
        proj = None
        for g, dil in enumerate(dilations):
            tiles = [q0 + g, k0 + g, v0 + g]
            if dil == 1:
                tiles += [za, zf0, zf0 + 1] + [g0 + t for t in range(n_gate_tiles)]
            q_tile = 0
            if dil == FFT_P:
                tiles = [u0, u0 + 1] + tiles
                q_tile = 2
            pg = _inproj(hs[g].reshape(batch * seq, d), w, tiles, q_tile, cos_tab, sin_tab, seq, dil, tn=tn)
            cols = pg.shape[1]
            o_g, lse_g = _banded_attention(pg.reshape(batch * dil, seq // dil, cols), q_tile, n_heads)
            if dil == 1:
                proj = pg.reshape(batch, seq, cols)
                outs.append(o_g)
                lses.append(lse_g)
            else:
                outs.append(o_g.reshape(batch, dil, seq // dil, attn_width))
                lses.append(lse_g.reshape(batch, dil, seq // dil, LANES))
            if dil == FFT_P:
                a_four = _fourier_mix(pg.reshape(batch, dil, seq // dil, cols), 0, proj, 2,
                                      batch, seq, four_width)
        mixed = _branches(proj, outs, lses, a_four, w_branch_attn[layer].astype(BF16),
                          w_branch_fourier[layer].astype(BF16), gate_bias[layer], 3, 6 * tn, n_heads)
        x = _outproj(mixed.reshape(batch * seq, d), w_out[layer].astype(BF16), x.reshape(batch * seq, d),
                     final_norm_gain if layer == depth - 1 else None).reshape(batch, seq, d)
    return x
```

```python
import functools

import numpy as np
import jax
import jax.numpy as jnp
from jax import lax
from jax.experimental import pallas as pl
from jax.experimental.pallas import tpu as pltpu

HEAD_DIM = 128
DILATED_GROUPS = ((128, 1), (512, 4), (2048, 16))
N_GROUPS = len(DILATED_GROUPS)
FOURIER_GROUPS = 8
ROPE_THETA = 10000.0
NORM_EPS = 1e-6
NEG_INF = -1e30
ATTN_RADIUS = 64
FFT_P = 16
FFT_F2_BLOCK = 16
LANES = 128
VMEM_LIMIT = 56 * 1024 * 1024

F32 = jnp.float32
BF16 = jnp.bfloat16


def _params(sem, vmem=VMEM_LIMIT):
    return pltpu.CompilerParams(dimension_semantics=sem, vmem_limit_bytes=vmem)


def _rmsnorm_kernel(x_ref, g_ref, *refs, dilations):
    o_ref = refs[0]
    x = x_ref[...].astype(F32)
    ms = jnp.mean(x * x, axis=-1, keepdims=True)
    y = x * lax.rsqrt(ms + NORM_EPS) * g_ref[...]
    o_ref[...] = y.astype(o_ref.dtype)
    if dilations:
        ybuf = refs[-1]
        rows = y.shape[0]
        chunks = [slice(c * LANES, (c + 1) * LANES) for c in range(y.shape[1] // LANES)]
        for c, sl in enumerate(chunks):
            ybuf[c] = y[:, sl]
        for d, od_ref in zip(dilations, refs[1:-1]):
            for r in range(d):
                for c, sl in enumerate(chunks):
                    od_ref[r, :, sl] = ybuf[c, pl.ds(r, rows // d, stride=d), :].astype(od_ref.dtype)


def _rmsnorm(x, gain, out_dtype, dilations=(), tm=512):
    batch, seq, d = x.shape
    out_specs = [pl.BlockSpec((None, tm, d), lambda b, i: (b, i, 0))]
    out_shape = [jax.ShapeDtypeStruct((batch, seq, d), out_dtype)]
    for dil in dilations:
        out_specs.append(pl.BlockSpec((None, dil, tm // dil, d), lambda b, i: (b, 0, i, 0)))
        out_shape.append(jax.ShapeDtypeStruct((batch, dil, seq // dil, d), out_dtype))
    return pl.pallas_call(
        functools.partial(_rmsnorm_kernel, dilations=tuple(dilations)),
        grid=(batch, seq // tm),
        in_specs=[pl.BlockSpec((None, tm, d), lambda b, i: (b, i, 0)),
                  pl.BlockSpec((1, d), lambda b, i: (0, 0))],
        out_specs=out_specs,
        out_shape=out_shape,
        scratch_shapes=[pltpu.VMEM((d // LANES, tm, LANES), F32)] if dilations else [],
        compiler_params=_params(("parallel", "parallel")),
        name="rmsnorm",
    )(x, gain.reshape(1, d).astype(F32))


def _inproj_kernel(tiles_ref, h_ref, w_ref, cos0_ref, cos1_ref, sin0_ref, sin1_ref, o_ref, *, rot_lo):
    del tiles_ref
    j = pl.program_id(1)
    is_rot = (j >= rot_lo) & (j < rot_lo + 2)
    tm, tn = o_ref.shape
    half = tm // 2
    chunk = 2 * HEAD_DIM

    @pl.when(is_rot)
    def _():
        scale = jnp.where(j == rot_lo, HEAD_DIM ** -0.5, 1.0).astype(F32)
        tabs = [(cos0_ref[...] * scale, sin0_ref[...] * scale), (cos1_ref[...] * scale, sin1_ref[...] * scale)]
        for ck in range(tn // chunk):
            acc = jnp.dot(h_ref[...], w_ref[:, ck * chunk:(ck + 1) * chunk], preferred_element_type=F32)
            for a, (c, s) in enumerate(tabs):
                rows = slice(a * half, (a + 1) * half)
                for hd in range(chunk // HEAD_DIM):
                    xh = acc[rows, hd * HEAD_DIM:(hd + 1) * HEAD_DIM]
                    lo = ck * chunk + hd * HEAD_DIM
                    o_ref[rows, lo:lo + HEAD_DIM] = (
                        xh * c + pltpu.roll(xh, HEAD_DIM // 2, axis=1) * s).astype(o_ref.dtype)

    @pl.when(jnp.logical_not(is_rot))
    def _():
        o_ref[...] = jnp.dot(h_ref[...], w_ref[...], preferred_element_type=F32).astype(o_ref.dtype)


def _inproj(h, w, col_tiles, rot_lo, cos_tab, sin_tab, seq, dilation, tm=1024, tn=1024):
    m, k = h.shape
    sub = seq // dilation
    half = tm // 2
    assert sub % half == 0
    cos_v = cos_tab.reshape(sub, dilation * HEAD_DIM)
    sin_v = sin_tab.reshape(sub, dilation * HEAD_DIM)

    def tab_spec(a):
        def index(i, j, t):
            pos = (i * tm + a * half) % seq
            return ((pos % sub) // half, pos // sub)
        return pl.BlockSpec((half, HEAD_DIM), index)

    grid_spec = pltpu.PrefetchScalarGridSpec(
        num_scalar_prefetch=1,
        grid=(m // tm, len(col_tiles)),
        in_specs=[pl.BlockSpec((tm, k), lambda i, j, t: (i, 0)),
                  pl.BlockSpec((k, tn), lambda i, j, t: (0, t[j])),
                  tab_spec(0), tab_spec(1), tab_spec(0), tab_spec(1)],
        out_specs=pl.BlockSpec((tm, tn), lambda i, j, t: (i, j)),
    )
    return pl.pallas_call(
        functools.partial(_inproj_kernel, rot_lo=rot_lo),
        grid_spec=grid_spec,
        out_shape=jax.ShapeDtypeStruct((m, len(col_tiles) * tn), BF16),
        compiler_params=_params(("parallel", "arbitrary")),
        name="inproj",
    )(jnp.asarray(np.asarray(col_tiles, np.int32)), h, w, cos_v, cos_v, sin_v, sin_v)


def _attn_kernel(q_ref, kp_ref, kc_ref, kn_ref, vp_ref, vc_ref, vn_ref, o_ref, lse_ref,
                 kbuf, vbuf, *, tq, sub, n_heads):
    it = pl.program_id(1)
    r = ATTN_RADIUS
    qb = 2 * r
    kb = 4 * r
    kbuf[0:r, :] = kp_ref[...]
    kbuf[r:r + tq, :] = kc_ref[...]
    kbuf[r + tq:, :] = kn_ref[...]
    vbuf[0:r, :] = vp_ref[...]
    vbuf[r:r + tq, :] = vc_ref[...]
    vbuf[r + tq:, :] = vn_ref[...]

    def body(i, carry):
        r0 = pl.multiple_of(i * qb, qb)
        row = lax.broadcasted_iota(jnp.int32, (qb, kb), 0)
        col = lax.broadcasted_iota(jnp.int32, (qb, kb), 1)
        delta = col - row
        kpos = col + (it * tq + i * qb - r)
        valid = (delta >= 0) & (delta <= 2 * r) & (kpos >= 0) & (kpos < sub)
        lane = lax.broadcasted_iota(jnp.int32, (qb, LANES), 1)
        lse_tile = jnp.zeros((qb, LANES), F32)
        for h in range(n_heads):
            sl = slice(h * HEAD_DIM, (h + 1) * HEAD_DIM)
            q = q_ref[pl.ds(r0, qb), sl]
            k = kbuf[pl.ds(r0, kb), sl]
            v = vbuf[pl.ds(r0, kb), sl]
            s = lax.dot_general(q, k, (((1,), (1,)), ((), ())), preferred_element_type=F32)
            s = jnp.where(valid, s, NEG_INF)
            m = jnp.max(s, axis=-1, keepdims=True)
            p = jnp.exp(s - m)
            l = jnp.sum(p, axis=-1, keepdims=True)
            o = jnp.dot(p.astype(BF16), v, preferred_element_type=F32)
            o_ref[pl.ds(r0, qb), sl] = (o / l).astype(o_ref.dtype)
            lse_tile = jnp.where(lane == h, m + jnp.log(l), lse_tile)
        lse_ref[pl.ds(r0, qb), :] = lse_tile
        return carry

    lax.fori_loop(0, tq // qb, body, 0, unroll=True)


def _banded_attention(qkv, q_tile, n_heads, tq=512):
    n, sub, _ = qkv.shape
    width = n_heads * HEAD_DIM
    tq = min(tq, sub)
    r = ATTN_RADIUS
    per = tq // r

    def cur(off):
        return pl.BlockSpec((None, tq, width), lambda b, t: (b, t, q_tile + off))

    def prev(off):
        return pl.BlockSpec((None, r, width), lambda b, t: (b, jnp.maximum(t * per - 1, 0), q_tile + off))

    def nxt(off):
        return pl.BlockSpec((None, r, width),
                            lambda b, t: (b, jnp.minimum((t + 1) * per, sub // r - 1), q_tile + off))

    return pl.pallas_call(
        functools.partial(_attn_kernel, tq=tq, sub=sub, n_heads=n_heads),
        grid=(n, sub // tq),
        in_specs=[cur(0), prev(1), cur(1), nxt(1), prev(2), cur(2), nxt(2)],
        out_specs=[pl.BlockSpec((None, tq, width), lambda b, t: (b, t, 0)),
                   pl.BlockSpec((None, tq, LANES), lambda b, t: (b, t, 0))],
        out_shape=[jax.ShapeDtypeStruct((n, sub, width), BF16),
                   jax.ShapeDtypeStruct((n, sub, LANES), F32)],
        scratch_shapes=[pltpu.VMEM((tq + 2 * r, width), BF16),
                        pltpu.VMEM((tq + 2 * r, width), BF16)],
        compiler_params=_params(("parallel", "parallel")),
        name=f"attn_sub{sub}",
    )(qkv, qkv, qkv, qkv, qkv, qkv, qkv)


def _dft_tables(seq, gdim):
    p, fb = FFT_P, FFT_F2_BLOCK
    q = seq // p
    c = np.arange(gdim, dtype=np.float64)
    ang = 2 * np.pi * np.outer(c, c) / gdim
    wc = np.concatenate([np.cos(ang), np.sin(ang)], axis=1) / np.sqrt(gdim)
    aq = 2 * np.pi * (np.outer(np.arange(q), np.arange(q)) % q) / q
    cq, sq = np.cos(aq) / np.sqrt(q), np.sin(aq) / np.sqrt(q)
    w1 = np.block([[cq, -sq], [-sq, -cq]])
    f = (q * np.arange(p)[None, :, None, None]
         + fb * np.arange(q // fb)[:, None, None, None] + np.arange(fb)[None, None, :, None])
    phi = 2 * np.pi * ((f * np.arange(p)[None, None, None, :]) % seq) / seq
    eye = np.eye(fb)
    parts = [np.einsum('afjs,jk->afjsk', t / np.sqrt(p), eye) for t in (np.cos(phi), np.sin(phi))]
    big = np.stack(parts, axis=3).reshape(q // fb, p * fb, 2 * p * fb)
    as_bf16 = lambda a: jnp.asarray(a.astype(np.float32)).astype(BF16)
    return as_bf16(wc), as_bf16(w1), as_bf16(big)


def _fft1_kernel(u_ref, wc_ref, w1_ref, h_ref, *, gdim):
    q = u_ref.shape[0]
    for g in range(u_ref.shape[1] // gdim):
        sl = slice(g * gdim, (g + 1) * gdim)
        ab = jnp.dot(u_ref[:, sl], wc_ref[...], preferred_element_type=F32)
        stack = jnp.concatenate([ab[:, :gdim], ab[:, gdim:]], axis=0).astype(BF16)
        hh = jnp.dot(w1_ref[...], stack, preferred_element_type=F32)
        h_ref[0, :, sl] = hh[:q].astype(h_ref.dtype)
        h_ref[1, :, sl] = hh[q:].astype(h_ref.dtype)


def _sigmoid(z):
    return 0.5 * jnp.tanh(0.5 * z) + 0.5


def _silu(z):
    return z * _sigmoid(z)


def _fft2_kernel(h_ref, big_ref, z_ref, o_ref):
    parts, p, fb, width = h_ref.shape
    rhs = h_ref[...].reshape(parts * p * fb, width)
    y = jnp.dot(big_ref[...], rhs, preferred_element_type=F32)
    o_ref[...] = (y.reshape(p, fb, width) * _silu(z_ref[...].astype(F32))).astype(o_ref.dtype)


def _fourier_mix(u_cm, u_tile, proj, z_tile, batch, seq, width):
    gdim = width // FOURIER_GROUPS
    p, fb = FFT_P, FFT_F2_BLOCK
    q = seq // p
    wc, w1, big = _dft_tables(seq, gdim)
    h = pl.pallas_call(
        functools.partial(_fft1_kernel, gdim=gdim),
        grid=(batch, p),
        in_specs=[pl.BlockSpec((None, None, q, width), lambda b, s: (b, s, 0, u_tile)),
                  pl.BlockSpec(wc.shape, lambda b, s: (0, 0)),
                  pl.BlockSpec(w1.shape, lambda b, s: (0, 0))],
        out_specs=pl.BlockSpec((None, 2, None, q, width), lambda b, s: (b, 0, s, 0, 0)),
        out_shape=jax.ShapeDtypeStruct((batch, 2, p, q, width), BF16),
        compiler_params=_params(("parallel", "parallel")),
        name="fft_seq1",
    )(u_cm, wc, w1)
    y = pl.pallas_call(
        _fft2_kernel,
        grid=(batch, q // fb),
        in_specs=[pl.BlockSpec((None, 2, p, fb, width), lambda b, a: (b, 0, 0, a, 0)),
                  pl.BlockSpec((None, p * fb, 2 * p * fb), lambda b, a: (a, 0, 0)),
                  pl.BlockSpec((None, p, None, fb, width), lambda b, a: (b, 0, a, 0, z_tile))],
        out_specs=pl.BlockSpec((None, p, None, fb, width), lambda b, a: (b, 0, a, 0, 0)),
        out_shape=jax.ShapeDtypeStruct((batch, p, q // fb, fb, width), BF16),
        compiler_params=_params(("parallel", "parallel")),
        name="fft_seq2",
    )(h, big, proj.reshape(batch, p, q // fb, fb, proj.shape[-1]))
    return y.reshape(batch, seq, width)


def _branch_kernel(o1_ref, o2_ref, o3_ref, l1_ref, l2_ref, l3_ref, za_ref, af_ref,
                   wa_ref, wf_ref, ga_ref, gf_ref, ba_ref, bf_ref, out_ref,
                   a_attn, on2, on3, ln2, ln3, *, n_heads):
    @pl.when(pl.program_id(2) == 0)
    def _():
        rows = ln2.shape[0]
        for src, dst in ((o2_ref, on2), (o3_ref, on3)):
            d = src.shape[0]
            for r in range(d):
                for h in range(n_heads):
                    dst[h, pl.ds(r, rows // d, stride=d), :] = (
                        src[r, :, h * HEAD_DIM:(h + 1) * HEAD_DIM].astype(F32))
        for src, dst in ((l2_ref, ln2), (l3_ref, ln3)):
            d = src.shape[0]
            for r in range(d):
                dst[pl.ds(r, rows // d, stride=d), :] = src[r]
        l1, l2, l3 = l1_ref[...], ln2[...], ln3[...]
        m = jnp.maximum(jnp.maximum(l1, l2), l3)
        e1, e2, e3 = jnp.exp(l1 - m), jnp.exp(l2 - m), jnp.exp(l3 - m)
        inv = 1.0 / (e1 + e2 + e3)
        al1, al2, al3 = e1 * inv, e2 * inv, e3 * inv
        for h in range(n_heads):
            sl = slice(h * HEAD_DIM, (h + 1) * HEAD_DIM)
            o = (al1[:, h:h + 1] * o1_ref[:, sl].astype(F32)
                 + al2[:, h:h + 1] * on2[h]
                 + al3[:, h:h + 1] * on3[h])
            a_attn[:, sl] = (o * _silu(za_ref[:, sl].astype(F32))).astype(a_attn.dtype)

    br_a = jnp.dot(a_attn[...], wa_ref[...], preferred_element_type=F32)
    br_f = jnp.dot(af_ref[...], wf_ref[...], preferred_element_type=F32)
    g_a = _sigmoid(ga_ref[...].astype(F32) + ba_ref[...])
    g_f = _sigmoid(gf_ref[...].astype(F32) + bf_ref[...])
    out_ref[...] = (g_a * br_a + g_f * br_f).astype(out_ref.dtype)


def _branches(proj, outs, lses, a_four, w_attn, w_four, gate_bias, za_tile, gate_off,
              n_heads, tm=1024, tn=512):
    batch, seq, _ = proj.shape
    aw, fw = w_attn.shape[0], w_four.shape[0]
    d = w_attn.shape[1]
    assert gate_off % tn == 0
    gate_tile = gate_off // tn
    nj = d // tn
    bias = gate_bias.reshape(1, 2 * d).astype(F32)

    def nat(width, tile=0):
        return pl.BlockSpec((None, tm, width), lambda b, i, j: (b, i, tile))

    def cm(arr):
        dil = arr.shape[1]
        return pl.BlockSpec((None, dil, tm // dil, arr.shape[3]), lambda b, i, j: (b, 0, i, 0))

    return pl.pallas_call(
        functools.partial(_branch_kernel, n_heads=n_heads),
        grid=(batch, seq // tm, nj),
        in_specs=[nat(aw), cm(outs[1]), cm(outs[2]), nat(LANES), cm(lses[1]), cm(lses[2]),
                  nat(aw, za_tile), nat(fw),
                  pl.BlockSpec((aw, tn), lambda b, i, j: (0, j)),
                  pl.BlockSpec((fw, tn), lambda b, i, j: (0, j)),
                  pl.BlockSpec((None, tm, tn), lambda b, i, j: (b, i, gate_tile + j)),
                  pl.BlockSpec((None, tm, tn), lambda b, i, j: (b, i, gate_tile + nj + j)),
                  pl.BlockSpec((1, tn), lambda b, i, j: (0, j)),
                  pl.BlockSpec((1, tn), lambda b, i, j: (0, nj + j))],
        out_specs=pl.BlockSpec((None, tm, tn), lambda b, i, j: (b, i, j)),
        out_shape=jax.ShapeDtypeStruct((batch, seq, d), BF16),
        scratch_shapes=[pltpu.VMEM((tm, aw), BF16),
                        pltpu.VMEM((n_heads, tm, HEAD_DIM), F32), pltpu.VMEM((n_heads, tm, HEAD_DIM), F32),
                        pltpu.VMEM((tm, LANES), F32), pltpu.VMEM((tm, LANES), F32)],
        compiler_params=_params(("parallel", "parallel", "arbitrary")),
        name="branches",
    )(outs[0], outs[1], outs[2], lses[0], lses[1], lses[2], proj, a_four,
      w_attn, w_four, proj, proj, bias, bias)


def _outproj_kernel(a_ref, w_ref, x_ref, *refs, nj, final_norm):
    o_ref = refs[-1]
    j = pl.program_id(1)
    tn = w_ref.shape[1]
    for jj in range(nj):
        @pl.when(j == jj)
        def _(jj=jj):
            o_ref[:, jj * tn:(jj + 1) * tn] = x_ref[...] + jnp.dot(
                a_ref[...], w_ref[...], preferred_element_type=F32)

    if final_norm:
        @pl.when(j == nj - 1)
        def _():
            chunk = 128
            for c in range(o_ref.shape[0] // chunk):
                rows = slice(c * chunk, (c + 1) * chunk)
                v = o_ref[rows, :]
                ms = jnp.mean(v * v, axis=-1, keepdims=True)
                o_ref[rows, :] = v * lax.rsqrt(ms + NORM_EPS) * refs[0][...]


def _outproj(a, w, x, final_gain=None, tm=512, tn=1024):
    m, k = a.shape
    n = w.shape[1]
    nj = n // tn
    operands = [a, w, x]
    in_specs = [pl.BlockSpec((tm, k), lambda i, j: (i, 0)),
                pl.BlockSpec((k, tn), lambda i, j: (0, j)),
                pl.BlockSpec((tm, tn), lambda i, j: (i, j))]
    if final_gain is not None:
        operands.append(final_gain.reshape(1, n).astype(F32))
        in_specs.append(pl.BlockSpec((1, n), lambda i, j: (0, 0)))
    return pl.pallas_call(
        functools.partial(_outproj_kernel, nj=nj, final_norm=final_gain is not None),
        grid=(m // tm, nj),
        in_specs=in_specs,
        out_specs=pl.BlockSpec((tm, n), lambda i, j: (i, 0)),
        out_shape=jax.ShapeDtypeStruct((m, n), F32),
        compiler_params=_params(("parallel", "arbitrary")),
        name="outproj",
    )(*operands)


def _rotary_tables(seq):
    half = HEAD_DIM // 2
    inv_freq = ROPE_THETA ** (-jnp.arange(half, dtype=F32) * (2.0 / HEAD_DIM))
    ang = jnp.arange(seq, dtype=jnp.int32).astype(F32)[:, None] * inv_freq[None, :]
    cos, sin = jnp.cos(ang), jnp.sin(ang)
    return jnp.concatenate([cos, cos], axis=-1), jnp.concatenate([-sin, sin], axis=-1)


def kernel(x, norm_gain, w_in, gate_bias, w_branch_attn, w_branch_fourier, w_out, final_norm_gain):
    batch, seq, d = x.shape
    depth = norm_gain.shape[0]
    attn_width = w_branch_attn.shape[1]
    four_width = w_branch_fourier.shape[1]
    n_heads = attn_width // HEAD_DIM
    tn = attn_width
    assert four_width == 2 * tn and d % tn == 0
    assert w_in.shape[2] == (3 * N_GROUPS + 1) * tn + 2 * four_width + 2 * d
    assert all(w // (2 * dil) == ATTN_RADIUS for w, dil in DILATED_GROUPS)
    dilations = tuple(dil for _, dil in DILATED_GROUPS)
    assert dilations[0] == 1 and FFT_P in dilations
    q0, k0, v0 = 0, N_GROUPS, 2 * N_GROUPS
    za = 3 * N_GROUPS
    u0 = za + 1
    zf0 = u0 + 2
    g0 = zf0 + 2
    n_gate_tiles = 2 * d // tn

    cos_tab, sin_tab = _rotary_tables(seq)
    for layer in range(depth):
        hs = _rmsnorm(x, norm_gain[layer], BF16, dilations[1:])
        w = w_in[layer].astype(BF16)
        outs, lses = [], []
        a_four = None
        proj = None
        for g, dil in enumerate(dilations):
            tiles = [q0 + g, k0 + g, v0 + g]
            if dil == 1:
                tiles += [za, zf0, zf0 + 1] + [g0 + t for t in range(n_gate_tiles)]
            q_tile = 0
            if dil == FFT_P:
                tiles = [u0, u0 + 1] + tiles
                q_tile = 2
            pg = _inproj(hs[g].reshape(batch * seq, d), w, tiles, q_tile, cos_tab, sin_tab, seq, dil, tn=tn)
            cols = pg.shape[1]
            o_g, lse_g = _banded_attention(pg.reshape(batch * dil, seq // dil, cols), q_tile, n_heads)
            if dil == 1:
                proj = pg.reshape(batch, seq, cols)
                outs.append(o_g)
                lses.append(lse_g)
            else:
                outs.append(o_g.reshape(batch, dil, seq // dil, attn_width))
                lses.append(lse_g.reshape(batch, dil, seq // dil, LANES))
            if dil == FFT_P:
                a_four = _fourier_mix(pg.reshape(batch, dil, seq // dil, cols), 0, proj, 2,
                                      batch, seq, four_width)
        mixed = _branches(proj, outs, lses, a_four, w_branch_attn[layer].astype(BF16),
                          w_branch_fourier[layer].astype(BF16), gate_bias[layer], 3, 6 * tn, n_heads)
        x = _outproj(mixed.reshape(batch * seq, d), w_out[layer].astype(BF16), x.reshape(batch * seq, d),
                     final_norm_gain if layer == depth - 1 else None).reshape(batch, seq, d)
    return x
```

```python
import functools

import numpy as np
import jax
import jax.numpy as jnp
from jax import lax
from jax.experimental import pallas as pl
from jax.experimental.pallas import tpu as pltpu

HEAD_DIM = 128
DILATED_GROUPS = ((128, 1), (512, 4), (2048, 16))
N_GROUPS = len(DILATED_GROUPS)
FOURIER_GROUPS = 8
ROPE_THETA = 10000.0
NORM_EPS = 1e-6
NEG_INF = -1e30
ATTN_RADIUS = 64
FFT_P = 16
FFT_F2_BLOCK = 16
LANES = 128
VMEM_LIMIT = 56 * 1024 * 1024

F32 = jnp.float32
BF16 = jnp.bfloat16


def _params(sem, vmem=VMEM_LIMIT):
    return pltpu.CompilerParams(dimension_semantics=sem, vmem_limit_bytes=vmem)


def _rmsnorm_kernel(x_ref, g_ref, *refs, dilations):
    o_ref = refs[0]
    x = x_ref[...].astype(F32)
    ms = jnp.mean(x * x, axis=-1, keepdims=True)
    y = x * lax.rsqrt(ms + NORM_EPS) * g_ref[...]
    o_ref[...] = y.astype(o_ref.dtype)
    if dilations:
        ybuf = refs[-1]
        rows = y.shape[0]
        chunks = [slice(c * LANES, (c + 1) * LANES) for c in range(y.shape[1] // LANES)]
        for c, sl in enumerate(chunks):
            ybuf[c] = y[:, sl]
        for d, od_ref in zip(dilations, refs[1:-1]):
            for r in range(d):
                for c, sl in enumerate(chunks):
                    od_ref[r, :, sl] = ybuf[c, pl.ds(r, rows // d, stride=d), :].astype(od_ref.dtype)


def _rmsnorm(x, gain, out_dtype, dilations=(), tm=512):
    batch, seq, d = x.shape
    out_specs = [pl.BlockSpec((None, tm, d), lambda b, i: (b, i, 0))]
    out_shape = [jax.ShapeDtypeStruct((batch, seq, d), out_dtype)]
    for dil in dilations:
        out_specs.append(pl.BlockSpec((None, dil, tm // dil, d), lambda b, i: (b, 0, i, 0)))
        out_shape.append(jax.ShapeDtypeStruct((batch, dil, seq // dil, d), out_dtype))
    return pl.pallas_call(
        functools.partial(_rmsnorm_kernel, dilations=tuple(dilations)),
        grid=(batch, seq // tm),
        in_specs=[pl.BlockSpec((None, tm, d), lambda b, i: (b, i, 0)),
                  pl.BlockSpec((1, d), lambda b, i: (0, 0))],
        out_specs=out_specs,
        out_shape=out_shape,
        scratch_shapes=[pltpu.VMEM((d // LANES, tm, LANES), F32)] if dilations else [],
        compiler_params=_params(("parallel", "parallel")),
        name="rmsnorm",
    )(x, gain.reshape(1, d).astype(F32))


def _inproj_kernel(tiles_ref, h_ref, w_ref, cos0_ref, cos1_ref, sin0_ref, sin1_ref, o_ref, *, rot_lo):
    del tiles_ref
    j = pl.program_id(1)
    is_rot = (j >= rot_lo) & (j < rot_lo + 2)
    tm, tn = o_ref.shape
    half = tm // 2
    chunk = 2 * HEAD_DIM

    @pl.when(is_rot)
    def _():
        scale = jnp.where(j == rot_lo, HEAD_DIM ** -0.5, 1.0).astype(F32)
        tabs = [(cos0_ref[...] * scale, sin0_ref[...] * scale), (cos1_ref[...] * scale, sin1_ref[...] * scale)]
        for ck in range(tn // chunk):
            acc = jnp.dot(h_ref[...], w_ref[:, ck * chunk:(ck + 1) * chunk], preferred_element_type=F32)
            for a, (c, s) in enumerate(tabs):
                rows = slice(a * half, (a + 1) * half)
                for hd in range(chunk // HEAD_DIM):
                    xh = acc[rows, hd * HEAD_DIM:(hd + 1) * HEAD_DIM]
                    lo = ck * chunk + hd * HEAD_DIM
                    o_ref[rows, lo:lo + HEAD_DIM] = (
                        xh * c + pltpu.roll(xh, HEAD_DIM // 2, axis=1) * s).astype(o_ref.dtype)

    @pl.when(jnp.logical_not(is_rot))
    def _():
        o_ref[...] = jnp.dot(h_ref[...], w_ref[...], preferred_element_type=F32).astype(o_ref.dtype)


def _inproj(h, w, col_tiles, rot_lo, cos_tab, sin_tab, seq, dilation, tm=1024, tn=1024):
    m, k = h.shape
    sub = seq // dilation
    half = tm // 2
    assert sub % half == 0
    cos_v = cos_tab.reshape(sub, dilation * HEAD_DIM)
    sin_v = sin_tab.reshape(sub, dilation * HEAD_DIM)

    def tab_spec(a):
        def index(i, j, t):
            pos = (i * tm + a * half) % seq
            return ((pos % sub) // half, pos // sub)
        return pl.BlockSpec((half, HEAD_DIM), index)

    grid_spec = pltpu.PrefetchScalarGridSpec(
        num_scalar_prefetch=1,
        grid=(m // tm, len(col_tiles)),
        in_specs=[pl.BlockSpec((tm, k), lambda i, j, t: (i, 0)),
                  pl.BlockSpec((k, tn), lambda i, j, t: (0, t[j])),
                  tab_spec(0), tab_spec(1), tab_spec(0), tab_spec(1)],
        out_specs=pl.BlockSpec((tm, tn), lambda i, j, t: (i, j)),
    )
    return pl.pallas_call(
        functools.partial(_inproj_kernel, rot_lo=rot_lo),
        grid_spec=grid_spec,
        out_shape=jax.ShapeDtypeStruct((m, len(col_tiles) * tn), BF16),
        compiler_params=_params(("parallel", "arbitrary")),
        name="inproj",
    )(jnp.asarray(np.asarray(col_tiles, np.int32)), h, w, cos_v, cos_v, sin_v, sin_v)


def _attn_kernel(q_ref, kp_ref, kc_ref, kn_ref, vp_ref, vc_ref, vn_ref, o_ref, lse_ref,
                 kbuf, vbuf, *, tq, sub, n_heads):
    it = pl.program_id(1)
    r = ATTN_RADIUS
    qb = 2 * r
    kb = 4 * r
    kbuf[0:r, :] = kp_ref[...]
    kbuf[r:r + tq, :] = kc_ref[...]
    kbuf[r + tq:, :] = kn_ref[...]
    vbuf[0:r, :] = vp_ref[...]
    vbuf[r:r + tq, :] = vc_ref[...]
    vbuf[r + tq:, :] = vn_ref[...]

    def body(i, carry):
        r0 = pl.multiple_of(i * qb, qb)
        row = lax.broadcasted_iota(jnp.int32, (qb, kb), 0)
        col = lax.broadcasted_iota(jnp.int32, (qb, kb), 1)
        delta = col - row
        kpos = col + (it * tq + i * qb - r)
        valid = (delta >= 0) & (delta <= 2 * r) & (kpos >= 0) & (kpos < sub)
        lane = lax.broadcasted_iota(jnp.int32, (qb, LANES), 1)
        lse_tile = jnp.zeros((qb, LANES), F32)
        for h in range(n_heads):
            sl = slice(h * HEAD_DIM, (h + 1) * HEAD_DIM)
            q = q_ref[pl.ds(r0, qb), sl]
            k = kbuf[pl.ds(r0, kb), sl]
            v = vbuf[pl.ds(r0, kb), sl]
            s = lax.dot_general(q, k, (((1,), (1,)), ((), ())), preferred_element_type=F32)
            s = jnp.where(valid, s, NEG_INF)
            m = jnp.max(s, axis=-1, keepdims=True)
            p = jnp.exp(s - m)
            l = jnp.sum(p, axis=-1, keepdims=True)
            o = jnp.dot(p.astype(BF16), v, preferred_element_type=F32)
            o_ref[pl.ds(r0, qb), sl] = (o / l).astype(o_ref.dtype)
            lse_tile = jnp.where(lane == h, m + jnp.log(l), lse_tile)
        lse_ref[pl.ds(r0, qb), :] = lse_tile
        return carry

    lax.fori_loop(0, tq // qb, body, 0, unroll=True)


def _banded_attention(qkv, q_tile, n_heads, tq=1024):
    n, sub, _ = qkv.shape
    width = n_heads * HEAD_DIM
    tq = min(tq, sub)
    r = ATTN_RADIUS
    per = tq // r

    def cur(off):
        return pl.BlockSpec((None, tq, width), lambda b, t: (b, t, q_tile + off))

    def prev(off):
        return pl.BlockSpec((None, r, width), lambda b, t: (b, jnp.maximum(t * per - 1, 0), q_tile + off))

    def nxt(off):
        return pl.BlockSpec((None, r, width),
                            lambda b, t: (b, jnp.minimum((t + 1) * per, sub // r - 1), q_tile + off))

    return pl.pallas_call(
        functools.partial(_attn_kernel, tq=tq, sub=sub, n_heads=n_heads),
        grid=(n, sub // tq),
        in_specs=[cur(0), prev(1), cur(1), nxt(1), prev(2), cur(2), nxt(2)],
        out_specs=[pl.BlockSpec((None, tq, width), lambda b, t: (b, t, 0)),
                   pl.BlockSpec((None, tq, LANES), lambda b, t: (b, t, 0))],
        out_shape=[jax.ShapeDtypeStruct((n, sub, width), BF16),
                   jax.ShapeDtypeStruct((n, sub, LANES), F32)],
        scratch_shapes=[pltpu.VMEM((tq + 2 * r, width), BF16),
                        pltpu.VMEM((tq + 2 * r, width), BF16)],
        compiler_params=_params(("parallel", "parallel")),
        name=f"attn_sub{sub}",
    )(qkv, qkv, qkv, qkv, qkv, qkv, qkv)


def _dft_tables(seq, gdim):
    p, fb = FFT_P, FFT_F2_BLOCK
    q = seq // p
    c = np.arange(gdim, dtype=np.float64)
    ang = 2 * np.pi * np.outer(c, c) / gdim
    wc = np.concatenate([np.cos(ang), np.sin(ang)], axis=1) / np.sqrt(gdim)
    aq = 2 * np.pi * (np.outer(np.arange(q), np.arange(q)) % q) / q
    cq, sq = np.cos(aq) / np.sqrt(q), np.sin(aq) / np.sqrt(q)
    w1 = np.stack([cq, -(sq + cq), sq - cq])
    f = (q * np.arange(p)[None, :, None, None]
         + fb * np.arange(q // fb)[:, None, None, None] + np.arange(fb)[None, None, :, None])
    phi = 2 * np.pi * ((f * np.arange(p)[None, None, None, :]) % seq) / seq
    eye = np.eye(fb)
    parts = [np.einsum('afjs,jk->afjsk', t / np.sqrt(p), eye) for t in (np.cos(phi), np.sin(phi))]
    big = np.stack(parts, axis=3).reshape(q // fb, p * fb, 2 * p * fb)
    as_bf16 = lambda a: jnp.asarray(a.astype(np.float32)).astype(BF16)
    return as_bf16(wc), as_bf16(w1), as_bf16(big)


def _fft1_kernel(u_ref, wc_ref, w1_ref, h_ref, *, gdim):
    for g in range(u_ref.shape[1] // gdim):
        sl = slice(g * gdim, (g + 1) * gdim)
        ab = jnp.dot(u_ref[:, sl], wc_ref[...], preferred_element_type=F32)
        a, b = ab[:, :gdim], ab[:, gdim:]
        k1 = jnp.dot(w1_ref[0], (a - b).astype(BF16), preferred_element_type=F32)
        k2 = jnp.dot(w1_ref[1], a.astype(BF16), preferred_element_type=F32)
        k3 = jnp.dot(w1_ref[2], b.astype(BF16), preferred_element_type=F32)
        h_ref[0, :, sl] = (k1 - k3).astype(h_ref.dtype)
        h_ref[1, :, sl] = (k1 + k2).astype(h_ref.dtype)


def _sigmoid(z):
    return 0.5 * jnp.tanh(0.5 * z) + 0.5


def _silu(z):
    h = 0.5 * z
    return h + h * jnp.tanh(h)


def _fft2_kernel(h_ref, big_ref, z_ref, o_ref):
    parts, p, fb, width = h_ref.shape
    rhs = h_ref[...].reshape(parts * p * fb, width)
    y = jnp.dot(big_ref[...], rhs, preferred_element_type=F32)
    o_ref[...] = (y.reshape(p, fb, width) * _silu(z_ref[...].astype(F32))).astype(o_ref.dtype)


def _fourier_mix(u_cm, u_tile, proj, z_tile, batch, seq, width):
    gdim = width // FOURIER_GROUPS
    p, fb = FFT_P, FFT_F2_BLOCK
    q = seq // p
    wc, w1, big = _dft_tables(seq, gdim)
    h = pl.pallas_call(
        functools.partial(_fft1_kernel, gdim=gdim),
        grid=(batch, p),
        in_specs=[pl.BlockSpec((None, None, q, width), lambda b, s: (b, s, 0, u_tile)),
                  pl.BlockSpec(wc.shape, lambda b, s: (0, 0)),
                  pl.BlockSpec(w1.shape, lambda b, s: (0, 0, 0))],
        out_specs=pl.BlockSpec((None, 2, None, q, width), lambda b, s: (b, 0, s, 0, 0)),
        out_shape=jax.ShapeDtypeStruct((batch, 2, p, q, width), BF16),
        compiler_params=_params(("parallel", "parallel")),
        name="fft_seq1",
    )(u_cm, wc, w1)
    y = pl.pallas_call(
        _fft2_kernel,
        grid=(batch, q // fb),
        in_specs=[pl.BlockSpec((None, 2, p, fb, width), lambda b, a: (b, 0, 0, a, 0)),
                  pl.BlockSpec((None, p * fb, 2 * p * fb), lambda b, a: (a, 0, 0)),
                  pl.BlockSpec((None, p, None, fb, width), lambda b, a: (b, 0, a, 0, z_tile))],
        out_specs=pl.BlockSpec((None, p, None, fb, width), lambda b, a: (b, 0, a, 0, 0)),
        out_shape=jax.ShapeDtypeStruct((batch, p, q // fb, fb, width), BF16),
        compiler_params=_params(("parallel", "parallel")),
        name="fft_seq2",
    )(h, big, proj.reshape(batch, p, q // fb, fb, proj.shape[-1]))
    return y.reshape(batch, seq, width)


def _branch_kernel(o1_ref, o2_ref, o3_ref, l1_ref, l2_ref, l3_ref, za_ref, af_ref,
                   wa_ref, wf_ref, ga_ref, gf_ref, ba_ref, bf_ref, out_ref,
                   a_attn, on2, on3, ln2, ln3, *, n_heads):
    @pl.when(pl.program_id(2) == 0)
    def _():
        rows = ln2.shape[0]
        for src, dst in ((o2_ref, on2), (o3_ref, on3)):
            d = src.shape[0]
            for r in range(d):
                for h in range(n_heads):
                    dst[h, pl.ds(r, rows // d, stride=d), :] = (
                        src[r, :, h * HEAD_DIM:(h + 1) * HEAD_DIM].astype(F32))
        for src, dst in ((l2_ref, ln2), (l3_ref, ln3)):
            d = src.shape[0]
            for r in range(d):
                dst[pl.ds(r, rows // d, stride=d), :] = src[r]
        l1, l2, l3 = l1_ref[...], ln2[...], ln3[...]
        m = jnp.maximum(jnp.maximum(l1, l2), l3)
        e1, e2, e3 = jnp.exp(l1 - m), jnp.exp(l2 - m), jnp.exp(l3 - m)
        inv = 1.0 / (e1 + e2 + e3)
        al1, al2, al3 = e1 * inv, e2 * inv, e3 * inv
        for h in range(n_heads):
            sl = slice(h * HEAD_DIM, (h + 1) * HEAD_DIM)
            o = (al1[:, h:h + 1] * o1_ref[:, sl].astype(F32)
                 + al2[:, h:h + 1] * on2[h]
                 + al3[:, h:h + 1] * on3[h])
            a_attn[:, sl] = (o * _silu(za_ref[:, sl].astype(F32))).astype(a_attn.dtype)

    br_a = jnp.dot(a_attn[...], wa_ref[...], preferred_element_type=F32)
    br_f = jnp.dot(af_ref[...], wf_ref[...], preferred_element_type=F32)
    g_a = _sigmoid(ga_ref[...].astype(F32) + ba_ref[...])
    g_f = _sigmoid(gf_ref[...].astype(F32) + bf_ref[...])
    out_ref[...] = (g_a * br_a + g_f * br_f).astype(out_ref.dtype)


def _branches(proj, outs, lses, a_four, w_attn, w_four, gate_bias, za_tile, gate_off,
              n_heads, tm=1024, tn=512):
    batch, seq, _ = proj.shape
    aw, fw = w_attn.shape[0], w_four.shape[0]
    d = w_attn.shape[1]
    assert gate_off % tn == 0
    gate_tile = gate_off // tn
    nj = d // tn
    bias = gate_bias.reshape(1, 2 * d).astype(F32)

    def nat(width, tile=0):
        return pl.BlockSpec((None, tm, width), lambda b, i, j: (b, i, tile))

    def cm(arr):
        dil = arr.shape[1]
        return pl.BlockSpec((None, dil, tm // dil, arr.shape[3]), lambda b, i, j: (b, 0, i, 0))

    return pl.pallas_call(
        functools.partial(_branch_kernel, n_heads=n_heads),
        grid=(batch, seq // tm, nj),
        in_specs=[nat(aw), cm(outs[1]), cm(outs[2]), nat(LANES), cm(lses[1]), cm(lses[2]),
                  nat(aw, za_tile), nat(fw),
                  pl.BlockSpec((aw, tn), lambda b, i, j: (0, j)),
                  pl.BlockSpec((fw, tn), lambda b, i, j: (0, j)),
                  pl.BlockSpec((None, tm, tn), lambda b, i, j: (b, i, gate_tile + j)),
                  pl.BlockSpec((None, tm, tn), lambda b, i, j: (b, i, gate_tile + nj + j)),
                  pl.BlockSpec((1, tn), lambda b, i, j: (0, j)),
                  pl.BlockSpec((1, tn), lambda b, i, j: (0, nj + j))],
        out_specs=pl.BlockSpec((None, tm, tn), lambda b, i, j: (b, i, j)),
        out_shape=jax.ShapeDtypeStruct((batch, seq, d), BF16),
        scratch_shapes=[pltpu.VMEM((tm, aw), BF16),
                        pltpu.VMEM((n_heads, tm, HEAD_DIM), F32), pltpu.VMEM((n_heads, tm, HEAD_DIM), F32),
                        pltpu.VMEM((tm, LANES), F32), pltpu.VMEM((tm, LANES), F32)],
        compiler_params=_params(("parallel", "parallel", "arbitrary")),
        name="branches",
    )(outs[0], outs[1], outs[2], lses[0], lses[1], lses[2], proj, a_four,
      w_attn, w_four, proj, proj, bias, bias)


def _outproj_kernel(a_ref, w_ref, x_ref, *refs, nj, final_norm):
    o_ref = refs[-1]
    j = pl.program_id(1)
    tn = w_ref.shape[1]
    for jj in range(nj):
        @pl.when(j == jj)
        def _(jj=jj):
            o_ref[:, jj * tn:(jj + 1) * tn] = x_ref[...] + jnp.dot(
                a_ref[...], w_ref[...], preferred_element_type=F32)

    if final_norm:
        @pl.when(j == nj - 1)
        def _():
            chunk = 128
            for c in range(o_ref.shape[0] // chunk):
                rows = slice(c * chunk, (c + 1) * chunk)
                v = o_ref[rows, :]
                ms = jnp.mean(v * v, axis=-1, keepdims=True)
                o_ref[rows, :] = v * lax.rsqrt(ms + NORM_EPS) * refs[0][...]


def _outproj(a, w, x, final_gain=None, tm=512, tn=1024):
    m, k = a.shape
    n = w.shape[1]
    nj = n // tn
    operands = [a, w, x]
    in_specs = [pl.BlockSpec((tm, k), lambda i, j: (i, 0)),
                pl.BlockSpec((k, tn), lambda i, j: (0, j)),
                pl.BlockSpec((tm, tn), lambda i, j: (i, j))]
    if final_gain is not None:
        operands.append(final_gain.reshape(1, n).astype(F32))
        in_specs.append(pl.BlockSpec((1, n), lambda i, j: (0, 0)))
    return pl.pallas_call(
        functools.partial(_outproj_kernel, nj=nj, final_norm=final_gain is not None),
        grid=(m // tm, nj),
        in_specs=in_specs,
        out_specs=pl.BlockSpec((tm, n), lambda i, j: (i, 0)),
        out_shape=jax.ShapeDtypeStruct((m, n), F32),
        compiler_params=_params(("parallel", "arbitrary")),
        name="outproj",
    )(*operands)


def _rotary_tables(seq):
    half = HEAD_DIM // 2
    inv_freq = ROPE_THETA ** (-jnp.arange(half, dtype=F32) * (2.0 / HEAD_DIM))
    ang = jnp.arange(seq, dtype=jnp.int32).astype(F32)[:, None] * inv_freq[None, :]
    cos, sin = jnp.cos(ang), jnp.sin(ang)
    return jnp.concatenate([cos, cos], axis=-1), jnp.concatenate([-sin, sin], axis=-1)


def kernel(x, norm_gain, w_in, gate_bias, w_branch_attn, w_branch_fourier, w_out, final_norm_gain):
    batch, seq, d = x.shape
    depth = norm_gain.shape[0]
    attn_width = w_branch_attn.shape[1]
    four_width = w_branch_fourier.shape[1]
    n_heads = attn_width // HEAD_DIM
    tn = attn_width
    assert four_width == 2 * tn and d % tn == 0
    assert w_in.shape[2] == (3 * N_GROUPS + 1) * tn + 2 * four_width + 2 * d
    assert all(w // (2 * dil) == ATTN_RADIUS for w, dil in DILATED_GROUPS)
    dilations = tuple(dil for _, dil in DILATED_GROUPS)
    assert dilations[0] == 1 and FFT_P in dilations
    q0, k0, v0 = 0, N_GROUPS, 2 * N_GROUPS
    za = 3 * N_GROUPS
    u0 = za + 1
    zf0 = u0 + 2
    g0 = zf0 + 2
    n_gate_tiles = 2 * d // tn

    cos_tab, sin_tab = _rotary_tables(seq)
    for layer in range(depth):
        hs = _rmsnorm(x, norm_gain[layer], BF16, dilations[1:])
        w = w_in[layer].astype(BF16)
        outs, lses = [], []
        a_four = None
        proj = None
        for g, dil in enumerate(dilations):
            tiles = [q0 + g, k0 + g, v0 + g]
            if dil == 1:
                tiles += [za, zf0, zf0 + 1] + [g0 + t for t in range(n_gate_tiles)]
            q_tile = 0
            if dil == FFT_P:
                tiles = [u0, u0 + 1] + tiles
                q_tile = 2
            pg = _inproj(hs[g].reshape(batch * seq, d), w, tiles, q_tile, cos_tab, sin_tab, seq, dil, tn=tn)
            cols = pg.shape[1]
            o_g, lse_g = _banded_attention(pg.reshape(batch * dil, seq // dil, cols), q_tile, n_heads)
            if dil == 1:
                proj = pg.reshape(batch, seq, cols)
                outs.append(o_g)
                lses.append(lse_g)
            else:
                outs.append(o_g.reshape(batch, dil, seq // dil, attn_width))
                lses.append(lse_g.reshape(batch, dil, seq // dil, LANES))
            if dil == FFT_P:
                a_four = _fourier_mix(pg.reshape(batch, dil, seq // dil, cols), 0, proj, 2,
                                      batch, seq, four_width)
        mixed = _branches(proj, outs, lses, a_four, w_branch_attn[layer].astype(BF16),
                          w_branch_fourier[layer].astype(BF16), gate_bias[layer], 3, 6 * tn, n_heads)
        x = _outproj(mixed.reshape(batch * seq, d), w_out[layer].astype(BF16), x.reshape(batch * seq, d),
                     final_norm_gain if layer == depth - 1 else None).reshape(batch, seq, d)
    return x
```

```python
import functools

import numpy as np
import jax
import jax.numpy as jnp
from jax import lax
from jax.experimental import pallas as pl
from jax.experimental.pallas import tpu as pltpu

HEAD_DIM = 128
DILATED_GROUPS = ((128, 1), (512, 4), (2048, 16))
N_GROUPS = len(DILATED_GROUPS)
FOURIER_GROUPS = 8
ROPE_THETA = 10000.0
NORM_EPS = 1e-6
NEG_INF = -1e30
ATTN_RADIUS = 64
FFT_P = 16
FFT_F2_BLOCK = 16
LANES = 128
VMEM_LIMIT = 56 * 1024 * 1024

F32 = jnp.float32
BF16 = jnp.bfloat16


def _params(sem, vmem=VMEM_LIMIT):
    return pltpu.CompilerParams(dimension_semantics=sem, vmem_limit_bytes=vmem)


def _rmsnorm_kernel(x_ref, g_ref, w_ref, o_ref, *refs, dilations):
    od_refs, wb_ref, ybuf = refs[:-2], refs[-2], refs[-1]
    x = x_ref[...].astype(F32)
    ms = jnp.mean(x * x, axis=-1, keepdims=True)
    y = x * lax.rsqrt(ms + NORM_EPS) * g_ref[...]
    o_ref[...] = y.astype(o_ref.dtype)
    rows = y.shape[0]
    chunks = [slice(c * LANES, (c + 1) * LANES) for c in range(y.shape[1] // LANES)]
    for c, sl in enumerate(chunks):
        ybuf[c] = y[:, sl]
    for d, od_ref in zip(dilations, od_refs):
        for r in range(d):
            for c, sl in enumerate(chunks):
                od_ref[r, :, sl] = ybuf[c, pl.ds(r, rows // d, stride=d), :].astype(od_ref.dtype)
    wb_ref[...] = w_ref[...].astype(wb_ref.dtype)


def _rmsnorm(x, gain, w, dilations, tm=256):
    batch, seq, d = x.shape
    steps = batch * (seq // tm)
    wk, wn = w.shape
    assert wk % steps == 0
    wr = wk // steps
    out_specs = [pl.BlockSpec((None, tm, d), lambda b, i: (b, i, 0))]
    out_shape = [jax.ShapeDtypeStruct((batch, seq, d), BF16)]
    for dil in dilations:
        out_specs.append(pl.BlockSpec((None, dil, tm // dil, d), lambda b, i: (b, 0, i, 0)))
        out_shape.append(jax.ShapeDtypeStruct((batch, dil, seq // dil, d), BF16))
    w_index = lambda b, i: (b * (seq // tm) + i, 0)
    out_specs.append(pl.BlockSpec((wr, wn), w_index))
    out_shape.append(jax.ShapeDtypeStruct((wk, wn), BF16))
    return pl.pallas_call(
        functools.partial(_rmsnorm_kernel, dilations=tuple(dilations)),
        grid=(batch, seq // tm),
        in_specs=[pl.BlockSpec((None, tm, d), lambda b, i: (b, i, 0)),
                  pl.BlockSpec((1, d), lambda b, i: (0, 0)),
                  pl.BlockSpec((wr, wn), w_index)],
        out_specs=out_specs,
        out_shape=out_shape,
        scratch_shapes=[pltpu.VMEM((d // LANES, tm, LANES), F32)],
        compiler_params=_params(("parallel", "parallel")),
        name="rmsnorm",
    )(x, gain.reshape(1, d).astype(F32), w)


def _inproj_kernel(tiles_ref, h_ref, w_ref, cos0_ref, cos1_ref, sin0_ref, sin1_ref, o_ref, *, rot_lo):
    del tiles_ref
    j = pl.program_id(1)
    is_rot = (j >= rot_lo) & (j < rot_lo + 2)
    tm, tn = o_ref.shape
    half = tm // 2
    chunk = 2 * HEAD_DIM

    @pl.when(is_rot)
    def _():
        scale = jnp.where(j == rot_lo, HEAD_DIM ** -0.5, 1.0).astype(F32)
        tabs = [(cos0_ref[...] * scale, sin0_ref[...] * scale), (cos1_ref[...] * scale, sin1_ref[...] * scale)]
        for ck in range(tn // chunk):
            acc = jnp.dot(h_ref[...], w_ref[:, ck * chunk:(ck + 1) * chunk], preferred_element_type=F32)
            for a, (c, s) in enumerate(tabs):
                rows = slice(a * half, (a + 1) * half)
                for hd in range(chunk // HEAD_DIM):
                    xh = acc[rows, hd * HEAD_DIM:(hd + 1) * HEAD_DIM]
                    lo = ck * chunk + hd * HEAD_DIM
                    o_ref[rows, lo:lo + HEAD_DIM] = (
                        xh * c + pltpu.roll(xh, HEAD_DIM // 2, axis=1) * s).astype(o_ref.dtype)

    @pl.when(jnp.logical_not(is_rot))
    def _():
        o_ref[...] = jnp.dot(h_ref[...], w_ref[...], preferred_element_type=F32).astype(o_ref.dtype)


def _inproj(h, w, col_tiles, rot_lo, cos_tab, sin_tab, seq, dilation, tm=1024, tn=1024):
    m, k = h.shape
    sub = seq // dilation
    half = tm // 2
    assert sub % half == 0
    cos_v = cos_tab.reshape(sub, dilation * HEAD_DIM)
    sin_v = sin_tab.reshape(sub, dilation * HEAD_DIM)

    def tab_spec(a):
        def index(i, j, t):
            pos = (i * tm + a * half) % seq
            return ((pos % sub) // half, pos // sub)
        return pl.BlockSpec((half, HEAD_DIM), index)

    grid_spec = pltpu.PrefetchScalarGridSpec(
        num_scalar_prefetch=1,
        grid=(m // tm, len(col_tiles)),
        in_specs=[pl.BlockSpec((tm, k), lambda i, j, t: (i, 0)),
                  pl.BlockSpec((k, tn), lambda i, j, t: (0, t[j])),
                  tab_spec(0), tab_spec(1), tab_spec(0), tab_spec(1)],
        out_specs=pl.BlockSpec((tm, tn), lambda i, j, t: (i, j)),
    )
    return pl.pallas_call(
        functools.partial(_inproj_kernel, rot_lo=rot_lo),
        grid_spec=grid_spec,
        out_shape=jax.ShapeDtypeStruct((m, len(col_tiles) * tn), BF16),
        compiler_params=_params(("parallel", "arbitrary")),
        name="inproj",
    )(jnp.asarray(np.asarray(col_tiles, np.int32)), h, w, cos_v, cos_v, sin_v, sin_v)


def _attn_kernel(q_ref, kp_ref, kc_ref, kn_ref, vp_ref, vc_ref, vn_ref, o_ref, lse_ref,
                 kbuf, vbuf, *, tq, sub, n_heads):
    it = pl.program_id(1)
    r = ATTN_RADIUS
    qb = 2 * r
    kb = 4 * r
    kbuf[0:r, :] = kp_ref[...]
    kbuf[r:r + tq, :] = kc_ref[...]
    kbuf[r + tq:, :] = kn_ref[...]
    vbuf[0:r, :] = vp_ref[...]
    vbuf[r:r + tq, :] = vc_ref[...]
    vbuf[r + tq:, :] = vn_ref[...]

    def body(i, carry):
        r0 = pl.multiple_of(i * qb, qb)
        row = lax.broadcasted_iota(jnp.int32, (qb, kb), 0)
        col = lax.broadcasted_iota(jnp.int32, (qb, kb), 1)
        delta = col - row
        kpos = col + (it * tq + i * qb - r)
        valid = (delta >= 0) & (delta <= 2 * r) & (kpos >= 0) & (kpos < sub)
        lane = lax.broadcasted_iota(jnp.int32, (qb, LANES), 1)
        lse_tile = jnp.zeros((qb, LANES), F32)
        for h in range(n_heads):
            sl = slice(h * HEAD_DIM, (h + 1) * HEAD_DIM)
            q = q_ref[pl.ds(r0, qb), sl]
            k = kbuf[pl.ds(r0, kb), sl]
            v = vbuf[pl.ds(r0, kb), sl]
            s = lax.dot_general(q, k, (((1,), (1,)), ((), ())), preferred_element_type=F32)
            s = jnp.where(valid, s, NEG_INF)
            m = jnp.max(s, axis=-1, keepdims=True)
            p = jnp.exp(s - m)
            l = jnp.sum(p, axis=-1, keepdims=True)
            o = jnp.dot(p.astype(BF16), v, preferred_element_type=F32)
            o_ref[pl.ds(r0, qb), sl] = (o / l).astype(o_ref.dtype)
            lse_tile = jnp.where(lane == h, m + jnp.log(l), lse_tile)
        lse_ref[pl.ds(r0, qb), :] = lse_tile
        return carry

    lax.fori_loop(0, tq // qb, body, 0, unroll=True)


def _banded_attention(qkv, q_tile, n_heads, tq=1024):
    n, sub, _ = qkv.shape
    width = n_heads * HEAD_DIM
    tq = min(tq, sub)
    r = ATTN_RADIUS
    per = tq // r

    def cur(off):
        return pl.BlockSpec((None, tq, width), lambda b, t: (b, t, q_tile + off))

    def prev(off):
        return pl.BlockSpec((None, r, width), lambda b, t: (b, jnp.maximum(t * per - 1, 0), q_tile + off))

    def nxt(off):
        return pl.BlockSpec((None, r, width),
                            lambda b, t: (b, jnp.minimum((t + 1) * per, sub // r - 1), q_tile + off))

    return pl.pallas_call(
        functools.partial(_attn_kernel, tq=tq, sub=sub, n_heads=n_heads),
        grid=(n, sub // tq),
        in_specs=[cur(0), prev(1), cur(1), nxt(1), prev(2), cur(2), nxt(2)],
        out_specs=[pl.BlockSpec((None, tq, width), lambda b, t: (b, t, 0)),
                   pl.BlockSpec((None, tq, LANES), lambda b, t: (b, t, 0))],
        out_shape=[jax.ShapeDtypeStruct((n, sub, width), BF16),
                   jax.ShapeDtypeStruct((n, sub, LANES), F32)],
        scratch_shapes=[pltpu.VMEM((tq + 2 * r, width), BF16),
                        pltpu.VMEM((tq + 2 * r, width), BF16)],
        compiler_params=_params(("parallel", "parallel")),
        name=f"attn_sub{sub}",
    )(qkv, qkv, qkv, qkv, qkv, qkv, qkv)


def _dft_tables(seq, gdim):
    p, fb = FFT_P, FFT_F2_BLOCK
    q = seq // p
    c = np.arange(gdim, dtype=np.float64)
    ang = 2 * np.pi * np.outer(c, c) / gdim
    wc = np.concatenate([np.cos(ang), np.sin(ang)], axis=1) / np.sqrt(gdim)
    aq = 2 * np.pi * (np.outer(np.arange(q), np.arange(q)) % q) / q
    cq, sq = np.cos(aq) / np.sqrt(q), np.sin(aq) / np.sqrt(q)
    w1 = np.stack([cq, -(sq + cq), sq - cq])
    f = (q * np.arange(p)[None, :, None, None]
         + fb * np.arange(q // fb)[:, None, None, None] + np.arange(fb)[None, None, :, None])
    phi = 2 * np.pi * ((f * np.arange(p)[None, None, None, :]) % seq) / seq
    eye = np.eye(fb)
    parts = [np.einsum('afjs,jk->afjsk', t / np.sqrt(p), eye) for t in (np.cos(phi), np.sin(phi))]
    big = np.stack(parts, axis=3).reshape(q // fb, p * fb, 2 * p * fb)
    as_bf16 = lambda a: jnp.asarray(a.astype(np.float32)).astype(BF16)
    return as_bf16(wc), as_bf16(w1), as_bf16(big)


def _fft1_kernel(u_ref, wc_ref, w1_ref, h_ref, *, gdim):
    for g in range(u_ref.shape[1] // gdim):
        sl = slice(g * gdim, (g + 1) * gdim)
        ab = jnp.dot(u_ref[:, sl], wc_ref[...], preferred_element_type=F32)
        a, b = ab[:, :gdim], ab[:, gdim:]
        k1 = jnp.dot(w1_ref[0], (a - b).astype(BF16), preferred_element_type=F32)
        k2 = jnp.dot(w1_ref[1], a.astype(BF16), preferred_element_type=F32)
        k3 = jnp.dot(w1_ref[2], b.astype(BF16), preferred_element_type=F32)
        h_ref[0, :, sl] = (k1 - k3).astype(h_ref.dtype)
        h_ref[1, :, sl] = (k1 + k2).astype(h_ref.dtype)


def _sigmoid(z):
    return 0.5 * jnp.tanh(0.5 * z) + 0.5


def _silu(z):
    h = 0.5 * z
    return h + h * jnp.tanh(h)


def _fft2_kernel(h_ref, big_ref, z_ref, o_ref):
    parts, p, fb, width = h_ref.shape
    rhs = h_ref[...].reshape(parts * p * fb, width)
    y = jnp.dot(big_ref[...], rhs, preferred_element_type=F32)
    o_ref[...] = (y.reshape(p, fb, width) * _silu(z_ref[...].astype(F32))).astype(o_ref.dtype)


def _fourier_mix(u_cm, u_tile, proj, z_tile, batch, seq, width):
    gdim = width // FOURIER_GROUPS
    p, fb = FFT_P, FFT_F2_BLOCK
    q = seq // p
    wc, w1, big = _dft_tables(seq, gdim)
    h = pl.pallas_call(
        functools.partial(_fft1_kernel, gdim=gdim),
        grid=(batch, p),
        in_specs=[pl.BlockSpec((None, None, q, width), lambda b, s: (b, s, 0, u_tile)),
                  pl.BlockSpec(wc.shape, lambda b, s: (0, 0)),
                  pl.BlockSpec(w1.shape, lambda b, s: (0, 0, 0))],
        out_specs=pl.BlockSpec((None, 2, None, q, width), lambda b, s: (b, 0, s, 0, 0)),
        out_shape=jax.ShapeDtypeStruct((batch, 2, p, q, width), BF16),
        compiler_params=_params(("parallel", "parallel")),
        name="fft_seq1",
    )(u_cm, wc, w1)
    y = pl.pallas_call(
        _fft2_kernel,
        grid=(batch, q // fb),
        in_specs=[pl.BlockSpec((None, 2, p, fb, width), lambda b, a: (b, 0, 0, a, 0)),
                  pl.BlockSpec((None, p * fb, 2 * p * fb), lambda b, a: (a, 0, 0)),
                  pl.BlockSpec((None, p, None, fb, width), lambda b, a: (b, 0, a, 0, z_tile))],
        out_specs=pl.BlockSpec((None, p, None, fb, width), lambda b, a: (b, 0, a, 0, 0)),
        out_shape=jax.ShapeDtypeStruct((batch, p, q // fb, fb, width), BF16),
        compiler_params=_params(("parallel", "parallel")),
        name="fft_seq2",
    )(h, big, proj.reshape(batch, p, q // fb, fb, proj.shape[-1]))
    return y.reshape(batch, seq, width)


def _branch_kernel(o1_ref, o2_ref, o3_ref, l1_ref, l2_ref, l3_ref, za_ref, af_ref,
                   wa_ref, wf_ref, ga_ref, gf_ref, ba_ref, bf_ref, out_ref,
                   a_attn, on2, on3, ln2, ln3, *, n_heads):
    def emit_tile():
        br_f = jnp.dot(af_ref[...], wf_ref[...], preferred_element_type=F32)
        br_a = jnp.dot(a_attn[...], wa_ref[...], preferred_element_type=F32)
        g_a = _sigmoid(ga_ref[...].astype(F32) + ba_ref[...])
        g_f = _sigmoid(gf_ref[...].astype(F32) + bf_ref[...])
        out_ref[...] = (g_a * br_a + g_f * br_f).astype(out_ref.dtype)

    first = pl.program_id(2) == 0

    @pl.when(first)
    def _():
        rows = ln2.shape[0]
        for src, dst in ((o2_ref, on2), (o3_ref, on3)):
            d = src.shape[0]
            for r in range(d):
                for h in range(n_heads):
                    dst[h, pl.ds(r, rows // d, stride=d), :] = (
                        src[r, :, h * HEAD_DIM:(h + 1) * HEAD_DIM].astype(F32))
        for src, dst in ((l2_ref, ln2), (l3_ref, ln3)):
            d = src.shape[0]
            for r in range(d):
                dst[pl.ds(r, rows // d, stride=d), :] = src[r]
        l1, l2, l3 = l1_ref[...], ln2[...], ln3[...]
        m = jnp.maximum(jnp.maximum(l1, l2), l3)
        e1, e2, e3 = jnp.exp(l1 - m), jnp.exp(l2 - m), jnp.exp(l3 - m)
        inv = 1.0 / (e1 + e2 + e3)
        al1, al2, al3 = e1 * inv, e2 * inv, e3 * inv
        for h in range(n_heads):
            sl = slice(h * HEAD_DIM, (h + 1) * HEAD_DIM)
            o = (al1[:, h:h + 1] * o1_ref[:, sl].astype(F32)
                 + al2[:, h:h + 1] * on2[h]
                 + al3[:, h:h + 1] * on3[h])
            a_attn[:, sl] = (o * _silu(za_ref[:, sl].astype(F32))).astype(a_attn.dtype)
        emit_tile()

    pl.when(jnp.logical_not(first))(emit_tile)


def _branches(proj, outs, lses, a_four, w_attn, w_four, gate_bias, za_tile, gate_off,
              n_heads, tm=1024, tn=512):
    batch, seq, _ = proj.shape
    aw, fw = w_attn.shape[0], w_four.shape[0]
    d = w_attn.shape[1]
    assert gate_off % tn == 0
    gate_tile = gate_off // tn
    nj = d // tn
    bias = gate_bias.reshape(1, 2 * d).astype(F32)

    def nat(width, tile=0):
        return pl.BlockSpec((None, tm, width), lambda b, i, j: (b, i, tile))

    def cm(arr):
        dil = arr.shape[1]
        return pl.BlockSpec((None, dil, tm // dil, arr.shape[3]), lambda b, i, j: (b, 0, i, 0))

    return pl.pallas_call(
        functools.partial(_branch_kernel, n_heads=n_heads),
        grid=(batch, seq // tm, nj),
        in_specs=[nat(aw), cm(outs[1]), cm(outs[2]), nat(LANES), cm(lses[1]), cm(lses[2]),
                  nat(aw, za_tile), nat(fw),
                  pl.BlockSpec((aw, tn), lambda b, i, j: (0, j)),
                  pl.BlockSpec((fw, tn), lambda b, i, j: (0, j)),
                  pl.BlockSpec((None, tm, tn), lambda b, i, j: (b, i, gate_tile + j)),
                  pl.BlockSpec((None, tm, tn), lambda b, i, j: (b, i, gate_tile + nj + j)),
                  pl.BlockSpec((1, tn), lambda b, i, j: (0, j)),
                  pl.BlockSpec((1, tn), lambda b, i, j: (0, nj + j))],
        out_specs=pl.BlockSpec((None, tm, tn), lambda b, i, j: (b, i, j)),
        out_shape=jax.ShapeDtypeStruct((batch, seq, d), BF16),
        scratch_shapes=[pltpu.VMEM((tm, aw), BF16),
                        pltpu.VMEM((n_heads, tm, HEAD_DIM), F32), pltpu.VMEM((n_heads, tm, HEAD_DIM), F32),
                        pltpu.VMEM((tm, LANES), F32), pltpu.VMEM((tm, LANES), F32)],
        compiler_params=_params(("parallel", "parallel", "arbitrary")),
        name="branches",
    )(outs[0], outs[1], outs[2], lses[0], lses[1], lses[2], proj, a_four,
      w_attn, w_four, proj, proj, bias, bias)


def _outproj_kernel(a_ref, w_ref, x_ref, *refs, nj, final_norm):
    o_ref = refs[-1]
    j = pl.program_id(1)
    tn = w_ref.shape[1]
    for jj in range(nj):
        @pl.when(j == jj)
        def _(jj=jj):
            o_ref[:, jj * tn:(jj + 1) * tn] = x_ref[...] + jnp.dot(
                a_ref[...], w_ref[...], preferred_element_type=F32)

    if final_norm:
        @pl.when(j == nj - 1)
        def _():
            chunk = 128
            for c in range(o_ref.shape[0] // chunk):
                rows = slice(c * chunk, (c + 1) * chunk)
                v = o_ref[rows, :]
                ms = jnp.mean(v * v, axis=-1, keepdims=True)
                o_ref[rows, :] = v * lax.rsqrt(ms + NORM_EPS) * refs[0][...]


def _outproj(a, w, x, final_gain=None, tm=512, tn=1024):
    m, k = a.shape
    n = w.shape[1]
    nj = n // tn
    operands = [a, w, x]
    in_specs = [pl.BlockSpec((tm, k), lambda i, j: (i, 0)),
                pl.BlockSpec((k, tn), lambda i, j: (0, j)),
                pl.BlockSpec((tm, tn), lambda i, j: (i, j))]
    if final_gain is not None:
        operands.append(final_gain.reshape(1, n).astype(F32))
        in_specs.append(pl.BlockSpec((1, n), lambda i, j: (0, 0)))
    return pl.pallas_call(
        functools.partial(_outproj_kernel, nj=nj, final_norm=final_gain is not None),
        grid=(m // tm, nj),
        in_specs=in_specs,
        out_specs=pl.BlockSpec((tm, n), lambda i, j: (i, 0)),
        out_shape=jax.ShapeDtypeStruct((m, n), F32),
        compiler_params=_params(("parallel", "arbitrary")),
        name="outproj",
    )(*operands)


def _rotary_tables(seq):
    half = HEAD_DIM // 2
    inv_freq = ROPE_THETA ** (-jnp.arange(half, dtype=F32) * (2.0 / HEAD_DIM))
    ang = jnp.arange(seq, dtype=jnp.int32).astype(F32)[:, None] * inv_freq[None, :]
    cos, sin = jnp.cos(ang), jnp.sin(ang)
    return jnp.concatenate([cos, cos], axis=-1), jnp.concatenate([-sin, sin], axis=-1)


def kernel(x, norm_gain, w_in, gate_bias, w_branch_attn, w_branch_fourier, w_out, final_norm_gain):
    batch, seq, d = x.shape
    depth = norm_gain.shape[0]
    attn_width = w_branch_attn.shape[1]
    four_width = w_branch_fourier.shape[1]
    n_heads = attn_width // HEAD_DIM
    tn = attn_width
    assert four_width == 2 * tn and d % tn == 0
    assert w_in.shape[2] == (3 * N_GROUPS + 1) * tn + 2 * four_width + 2 * d
    assert all(w // (2 * dil) == ATTN_RADIUS for w, dil in DILATED_GROUPS)
    dilations = tuple(dil for _, dil in DILATED_GROUPS)
    assert dilations[0] == 1 and FFT_P in dilations
    q0, k0, v0 = 0, N_GROUPS, 2 * N_GROUPS
    za = 3 * N_GROUPS
    u0 = za + 1
    zf0 = u0 + 2
    g0 = zf0 + 2
    n_gate_tiles = 2 * d // tn

    cos_tab, sin_tab = _rotary_tables(seq)
    for layer in range(depth):
        *hs, w = _rmsnorm(x, norm_gain[layer], w_in[layer], dilations[1:])
        outs, lses = [], []
        a_four = None
        proj = None
        for g, dil in enumerate(dilations):
            tiles = [q0 + g, k0 + g, v0 + g]
            if dil == 1:
                tiles += [za, zf0, zf0 + 1] + [g0 + t for t in range(n_gate_tiles)]
            q_tile = 0
            if dil == FFT_P:
                tiles = [u0, u0 + 1] + tiles
                q_tile = 2
            pg = _inproj(hs[g].reshape(batch * seq, d), w, tiles, q_tile, cos_tab, sin_tab, seq, dil, tn=tn)
            cols = pg.shape[1]
            o_g, lse_g = _banded_attention(pg.reshape(batch * dil, seq // dil, cols), q_tile, n_heads)
            if dil == 1:
                proj = pg.reshape(batch, seq, cols)
                outs.append(o_g)
                lses.append(lse_g)
            else:
                outs.append(o_g.reshape(batch, dil, seq // dil, attn_width))
                lses.append(lse_g.reshape(batch, dil, seq // dil, LANES))
            if dil == FFT_P:
                a_four = _fourier_mix(pg.reshape(batch, dil, seq // dil, cols), 0, proj, 2,
                                      batch, seq, four_width)
        mixed = _branches(proj, outs, lses, a_four, w_branch_attn[layer].astype(BF16),
                          w_branch_fourier[layer].astype(BF16), gate_bias[layer], 3, 6 * tn, n_heads)
        x = _outproj(mixed.reshape(batch * seq, d), w_out[layer].astype(BF16), x.reshape(batch * seq, d),
                     final_norm_gain if layer == depth - 1 else None).reshape(batch, seq, d)
    return x
```

```python
import functools

import numpy as np
import jax
import jax.numpy as jnp
from jax import lax
from jax.experimental import pallas as pl
from jax.experimental.pallas import tpu as pltpu

HEAD_DIM = 128
DILATED_GROUPS = ((128, 1), (512, 4), (2048, 16))
N_GROUPS = len(DILATED_GROUPS)
FOURIER_GROUPS = 8
ROPE_THETA = 10000.0
NORM_EPS = 1e-6
NEG_INF = -1e30
ATTN_RADIUS = 64
FFT_P = 16
FFT_F2_BLOCK = 16
LANES = 128
VMEM_LIMIT = 56 * 1024 * 1024

F32 = jnp.float32
BF16 = jnp.bfloat16


def _params(sem, vmem=VMEM_LIMIT):
    return pltpu.CompilerParams(dimension_semantics=sem, vmem_limit_bytes=vmem)


def _rmsnorm_kernel(x_ref, g_ref, w_ref, o_ref, *refs, dilations):
    od_refs, wb_ref, ybuf = refs[:-2], refs[-2], refs[-1]
    x = x_ref[...].astype(F32)
    ms = jnp.mean(x * x, axis=-1, keepdims=True)
    y = x * lax.rsqrt(ms + NORM_EPS) * g_ref[...]
    o_ref[...] = y.astype(o_ref.dtype)
    rows = y.shape[0]
    chunks = [slice(c * LANES, (c + 1) * LANES) for c in range(y.shape[1] // LANES)]
    for c, sl in enumerate(chunks):
        ybuf[c] = y[:, sl]
    for d, od_ref in zip(dilations, od_refs):
        for r in range(d):
            for c, sl in enumerate(chunks):
                od_ref[r, :, sl] = ybuf[c, pl.ds(r, rows // d, stride=d), :].astype(od_ref.dtype)
    wb_ref[...] = w_ref[...].astype(wb_ref.dtype)


def _rmsnorm(x, gain, w, dilations, tm=256):
    batch, seq, d = x.shape
    steps = batch * (seq // tm)
    wk, wn = w.shape
    assert wk % steps == 0
    wr = wk // steps
    out_specs = [pl.BlockSpec((None, tm, d), lambda b, i: (b, i, 0))]
    out_shape = [jax.ShapeDtypeStruct((batch, seq, d), BF16)]
    for dil in dilations:
        out_specs.append(pl.BlockSpec((None, dil, tm // dil, d), lambda b, i: (b, 0, i, 0)))
        out_shape.append(jax.ShapeDtypeStruct((batch, dil, seq // dil, d), BF16))
    w_index = lambda b, i: (b * (seq // tm) + i, 0)
    out_specs.append(pl.BlockSpec((wr, wn), w_index))
    out_shape.append(jax.ShapeDtypeStruct((wk, wn), BF16))
    return pl.pallas_call(
        functools.partial(_rmsnorm_kernel, dilations=tuple(dilations)),
        grid=(batch, seq // tm),
        in_specs=[pl.BlockSpec((None, tm, d), lambda b, i: (b, i, 0)),
                  pl.BlockSpec((1, d), lambda b, i: (0, 0)),
                  pl.BlockSpec((wr, wn), w_index)],
        out_specs=out_specs,
        out_shape=out_shape,
        scratch_shapes=[pltpu.VMEM((d // LANES, tm, LANES), F32)],
        compiler_params=_params(("parallel", "parallel")),
        name="rmsnorm",
    )(x, gain.reshape(1, d).astype(F32), w)


def _inproj_kernel(tiles_ref, h_ref, w_ref, cos0_ref, cos1_ref, sin0_ref, sin1_ref, o_ref, *, rot_lo):
    del tiles_ref
    j = pl.program_id(1)
    is_rot = (j >= rot_lo) & (j < rot_lo + 2)
    tm, tn = o_ref.shape
    half = tm // 2
    chunk = 2 * HEAD_DIM

    @pl.when(is_rot)
    def _():
        scale = jnp.where(j == rot_lo, HEAD_DIM ** -0.5, 1.0).astype(F32)
        tabs = [(cos0_ref[...] * scale, sin0_ref[...] * scale), (cos1_ref[...] * scale, sin1_ref[...] * scale)]
        for ck in range(tn // chunk):
            acc = jnp.dot(h_ref[...], w_ref[:, ck * chunk:(ck + 1) * chunk], preferred_element_type=F32)
            for a, (c, s) in enumerate(tabs):
                rows = slice(a * half, (a + 1) * half)
                for hd in range(chunk // HEAD_DIM):
                    xh = acc[rows, hd * HEAD_DIM:(hd + 1) * HEAD_DIM]
                    lo = ck * chunk + hd * HEAD_DIM
                    o_ref[rows, lo:lo + HEAD_DIM] = (
                        xh * c + pltpu.roll(xh, HEAD_DIM // 2, axis=1) * s).astype(o_ref.dtype)

    @pl.when(jnp.logical_not(is_rot))
    def _():
        o_ref[...] = jnp.dot(h_ref[...], w_ref[...], preferred_element_type=F32).astype(o_ref.dtype)


def _inproj(h, w, col_tiles, rot_lo, cos_tab, sin_tab, seq, dilation, tm=1024, tn=1024):
    m, k = h.shape
    sub = seq // dilation
    half = tm // 2
    assert sub % half == 0
    cos_v = cos_tab.reshape(sub, dilation * HEAD_DIM)
    sin_v = sin_tab.reshape(sub, dilation * HEAD_DIM)

    def tab_spec(a):
        def index(i, j, t):
            pos = (i * tm + a * half) % seq
            return ((pos % sub) // half, pos // sub)
        return pl.BlockSpec((half, HEAD_DIM), index)

    grid_spec = pltpu.PrefetchScalarGridSpec(
        num_scalar_prefetch=1,
        grid=(m // tm, len(col_tiles)),
        in_specs=[pl.BlockSpec((tm, k), lambda i, j, t: (i, 0)),
                  pl.BlockSpec((k, tn), lambda i, j, t: (0, t[j])),
                  tab_spec(0), tab_spec(1), tab_spec(0), tab_spec(1)],
        out_specs=pl.BlockSpec((tm, tn), lambda i, j, t: (i, j)),
    )
    return pl.pallas_call(
        functools.partial(_inproj_kernel, rot_lo=rot_lo),
        grid_spec=grid_spec,
        out_shape=jax.ShapeDtypeStruct((m, len(col_tiles) * tn), BF16),
        compiler_params=_params(("parallel", "arbitrary")),
        name="inproj",
    )(jnp.asarray(np.asarray(col_tiles, np.int32)), h, w, cos_v, cos_v, sin_v, sin_v)


def _attn_kernel(q_ref, kp_ref, kc_ref, kn_ref, vp_ref, vc_ref, vn_ref, w_ref, o_ref, lse_ref, wb_ref,
                 kbuf, vbuf, *, tq, sub, n_heads):
    wb_ref[...] = w_ref[...].astype(wb_ref.dtype)
    it = pl.program_id(1)
    r = ATTN_RADIUS
    qb = 2 * r
    kb = 4 * r
    kbuf[0:r, :] = kp_ref[...]
    kbuf[r:r + tq, :] = kc_ref[...]
    kbuf[r + tq:, :] = kn_ref[...]
    vbuf[0:r, :] = vp_ref[...]
    vbuf[r:r + tq, :] = vc_ref[...]
    vbuf[r + tq:, :] = vn_ref[...]

    def body(i, carry):
        r0 = pl.multiple_of(i * qb, qb)
        row = lax.broadcasted_iota(jnp.int32, (qb, kb), 0)
        col = lax.broadcasted_iota(jnp.int32, (qb, kb), 1)
        delta = col - row
        kpos = col + (it * tq + i * qb - r)
        valid = (delta >= 0) & (delta <= 2 * r) & (kpos >= 0) & (kpos < sub)
        lane = lax.broadcasted_iota(jnp.int32, (qb, LANES), 1)
        lse_tile = jnp.zeros((qb, LANES), F32)
        for h in range(n_heads):
            sl = slice(h * HEAD_DIM, (h + 1) * HEAD_DIM)
            q = q_ref[pl.ds(r0, qb), sl]
            k = kbuf[pl.ds(r0, kb), sl]
            v = vbuf[pl.ds(r0, kb), sl]
            s = lax.dot_general(q, k, (((1,), (1,)), ((), ())), preferred_element_type=F32)
            s = jnp.where(valid, s, NEG_INF)
            m = jnp.max(s, axis=-1, keepdims=True)
            p = jnp.exp(s - m)
            l = jnp.sum(p, axis=-1, keepdims=True)
            o = jnp.dot(p.astype(BF16), v, preferred_element_type=F32)
            o_ref[pl.ds(r0, qb), sl] = (o / l).astype(o_ref.dtype)
            lse_tile = jnp.where(lane == h, m + jnp.log(l), lse_tile)
        lse_ref[pl.ds(r0, qb), :] = lse_tile
        return carry

    lax.fori_loop(0, tq // qb, body, 0, unroll=True)


def _banded_attention(qkv, q_tile, n_heads, w, tq=1024):
    n, sub, _ = qkv.shape
    width = n_heads * HEAD_DIM
    tq = min(tq, sub)
    r = ATTN_RADIUS
    per = tq // r
    nt = sub // tq
    wk, wn = w.shape
    assert wk % (n * nt) == 0
    w_spec = pl.BlockSpec((wk // (n * nt), wn), lambda b, t: (b * nt + t, 0))

    def cur(off):
        return pl.BlockSpec((None, tq, width), lambda b, t: (b, t, q_tile + off))

    def prev(off):
        return pl.BlockSpec((None, r, width), lambda b, t: (b, jnp.maximum(t * per - 1, 0), q_tile + off))

    def nxt(off):
        return pl.BlockSpec((None, r, width),
                            lambda b, t: (b, jnp.minimum((t + 1) * per, sub // r - 1), q_tile + off))

    return pl.pallas_call(
        functools.partial(_attn_kernel, tq=tq, sub=sub, n_heads=n_heads),
        grid=(n, nt),
        in_specs=[cur(0), prev(1), cur(1), nxt(1), prev(2), cur(2), nxt(2), w_spec],
        out_specs=[pl.BlockSpec((None, tq, width), lambda b, t: (b, t, 0)),
                   pl.BlockSpec((None, tq, LANES), lambda b, t: (b, t, 0)),
                   w_spec],
        out_shape=[jax.ShapeDtypeStruct((n, sub, width), BF16),
                   jax.ShapeDtypeStruct((n, sub, LANES), F32),
                   jax.ShapeDtypeStruct((wk, wn), BF16)],
        scratch_shapes=[pltpu.VMEM((tq + 2 * r, width), BF16),
                        pltpu.VMEM((tq + 2 * r, width), BF16)],
        compiler_params=_params(("parallel", "parallel")),
        name=f"attn_sub{sub}",
    )(qkv, qkv, qkv, qkv, qkv, qkv, qkv, w)


def _dft_tables(seq, gdim):
    p, fb = FFT_P, FFT_F2_BLOCK
    q = seq // p
    c = np.arange(gdim, dtype=np.float64)
    ang = 2 * np.pi * np.outer(c, c) / gdim
    wc = np.concatenate([np.cos(ang), np.sin(ang)], axis=1) / np.sqrt(gdim)
    aq = 2 * np.pi * (np.outer(np.arange(q), np.arange(q)) % q) / q
    cq, sq = np.cos(aq) / np.sqrt(q), np.sin(aq) / np.sqrt(q)
    w1 = np.stack([cq, -(sq + cq), sq - cq])
    f = (q * np.arange(p)[None, :, None, None]
         + fb * np.arange(q // fb)[:, None, None, None] + np.arange(fb)[None, None, :, None])
    phi = 2 * np.pi * ((f * np.arange(p)[None, None, None, :]) % seq) / seq
    eye = np.eye(fb)
    parts = [np.einsum('afjs,jk->afjsk', t / np.sqrt(p), eye) for t in (np.cos(phi), np.sin(phi))]
    big = np.stack(parts, axis=3).reshape(q // fb, p * fb, 2 * p * fb)
    as_bf16 = lambda a: jnp.asarray(a.astype(np.float32)).astype(BF16)
    return as_bf16(wc), as_bf16(w1), as_bf16(big)


def _fft1_kernel(u_ref, wc_ref, w1_ref, h_ref, *, gdim):
    for g in range(u_ref.shape[1] // gdim):
        sl = slice(g * gdim, (g + 1) * gdim)
        ab = jnp.dot(u_ref[:, sl], wc_ref[...], preferred_element_type=F32)
        a, b = ab[:, :gdim], ab[:, gdim:]
        k1 = jnp.dot(w1_ref[0], (a - b).astype(BF16), preferred_element_type=F32)
        k2 = jnp.dot(w1_ref[1], a.astype(BF16), preferred_element_type=F32)
        k3 = jnp.dot(w1_ref[2], b.astype(BF16), preferred_element_type=F32)
        h_ref[0, :, sl] = (k1 - k3).astype(h_ref.dtype)
        h_ref[1, :, sl] = (k1 + k2).astype(h_ref.dtype)


def _sigmoid(z):
    return 0.5 * jnp.tanh(0.5 * z) + 0.5


def _silu(z):
    h = 0.5 * z
    return h + h * jnp.tanh(h)


def _fft2_kernel(h_ref, big_ref, z_ref, o_ref):
    parts, p, fb, width = h_ref.shape
    rhs = h_ref[...].reshape(parts * p * fb, width)
    y = jnp.dot(big_ref[...], rhs, preferred_element_type=F32)
    o_ref[...] = (y.reshape(p, fb, width) * _silu(z_ref[...].astype(F32))).astype(o_ref.dtype)


def _fourier_mix(u_cm, u_tile, proj, z_tile, batch, seq, width):
    gdim = width // FOURIER_GROUPS
    p, fb = FFT_P, FFT_F2_BLOCK
    q = seq // p
    wc, w1, big = _dft_tables(seq, gdim)
    h = pl.pallas_call(
        functools.partial(_fft1_kernel, gdim=gdim),
        grid=(batch, p),
        in_specs=[pl.BlockSpec((None, None, q, width), lambda b, s: (b, s, 0, u_tile)),
                  pl.BlockSpec(wc.shape, lambda b, s: (0, 0)),
                  pl.BlockSpec(w1.shape, lambda b, s: (0, 0, 0))],
        out_specs=pl.BlockSpec((None, 2, None, q, width), lambda b, s: (b, 0, s, 0, 0)),
        out_shape=jax.ShapeDtypeStruct((batch, 2, p, q, width), BF16),
        compiler_params=_params(("parallel", "parallel")),
        name="fft_seq1",
    )(u_cm, wc, w1)
    y = pl.pallas_call(
        _fft2_kernel,
        grid=(batch, q // fb),
        in_specs=[pl.BlockSpec((None, 2, p, fb, width), lambda b, a: (b, 0, 0, a, 0)),
                  pl.BlockSpec((None, p * fb, 2 * p * fb), lambda b, a: (a, 0, 0)),
                  pl.BlockSpec((None, p, None, fb, width), lambda b, a: (b, 0, a, 0, z_tile))],
        out_specs=pl.BlockSpec((None, p, None, fb, width), lambda b, a: (b, 0, a, 0, 0)),
        out_shape=jax.ShapeDtypeStruct((batch, p, q // fb, fb, width), BF16),
        compiler_params=_params(("parallel", "parallel")),
        name="fft_seq2",
    )(h, big, proj.reshape(batch, p, q // fb, fb, proj.shape[-1]))
    return y.reshape(batch, seq, width)


def _branch_kernel(o1_ref, o2_ref, o3_ref, l1_ref, l2_ref, l3_ref, za_ref, af_ref,
                   wa_ref, wf_ref, ga_ref, gf_ref, ba_ref, bf_ref, out_ref,
                   a_attn, on2, on3, ln2, ln3, *, n_heads):
    def emit_tile():
        br_f = jnp.dot(af_ref[...], wf_ref[...], preferred_element_type=F32)
        br_a = jnp.dot(a_attn[...], wa_ref[...], preferred_element_type=F32)
        g_a = _sigmoid(ga_ref[...].astype(F32) + ba_ref[...])
        g_f = _sigmoid(gf_ref[...].astype(F32) + bf_ref[...])
        out_ref[...] = (g_a * br_a + g_f * br_f).astype(out_ref.dtype)

    first = pl.program_id(2) == 0

    @pl.when(first)
    def _():
        rows = ln2.shape[0]
        for src, dst in ((o2_ref, on2), (o3_ref, on3)):
            d = src.shape[0]
            for r in range(d):
                for h in range(n_heads):
                    dst[h, pl.ds(r, rows // d, stride=d), :] = (
                        src[r, :, h * HEAD_DIM:(h + 1) * HEAD_DIM].astype(F32))
        for src, dst in ((l2_ref, ln2), (l3_ref, ln3)):
            d = src.shape[0]
            for r in range(d):
                dst[pl.ds(r, rows // d, stride=d), :] = src[r]
        l1, l2, l3 = l1_ref[...], ln2[...], ln3[...]
        m = jnp.maximum(jnp.maximum(l1, l2), l3)
        e1, e2, e3 = jnp.exp(l1 - m), jnp.exp(l2 - m), jnp.exp(l3 - m)
        inv = 1.0 / (e1 + e2 + e3)
        al1, al2, al3 = e1 * inv, e2 * inv, e3 * inv
        for h in range(n_heads):
            sl = slice(h * HEAD_DIM, (h + 1) * HEAD_DIM)
            o = (al1[:, h:h + 1] * o1_ref[:, sl].astype(F32)
                 + al2[:, h:h + 1] * on2[h]
                 + al3[:, h:h + 1] * on3[h])
            a_attn[:, sl] = (o * _silu(za_ref[:, sl].astype(F32))).astype(a_attn.dtype)
        emit_tile()

    pl.when(jnp.logical_not(first))(emit_tile)


def _branches(proj, outs, lses, a_four, w_attn, w_four, gate_bias, za_tile, gate_off,
              n_heads, tm=1024, tn=512):
    batch, seq, _ = proj.shape
    aw, fw = w_attn.shape[0], w_four.shape[0]
    d = w_attn.shape[1]
    assert gate_off % tn == 0
    gate_tile = gate_off // tn
    nj = d // tn
    bias = gate_bias.reshape(1, 2 * d).astype(F32)

    def nat(width, tile=0):
        return pl.BlockSpec((None, tm, width), lambda b, i, j: (b, i, tile))

    def cm(arr):
        dil = arr.shape[1]
        return pl.BlockSpec((None, dil, tm // dil, arr.shape[3]), lambda b, i, j: (b, 0, i, 0))

    return pl.pallas_call(
        functools.partial(_branch_kernel, n_heads=n_heads),
        grid=(batch, seq // tm, nj),
        in_specs=[nat(aw), cm(outs[1]), cm(outs[2]), nat(LANES), cm(lses[1]), cm(lses[2]),
                  nat(aw, za_tile), nat(fw),
                  pl.BlockSpec((aw, tn), lambda b, i, j: (0, j)),
                  pl.BlockSpec((fw, tn), lambda b, i, j: (0, j)),
                  pl.BlockSpec((None, tm, tn), lambda b, i, j: (b, i, gate_tile + j)),
                  pl.BlockSpec((None, tm, tn), lambda b, i, j: (b, i, gate_tile + nj + j)),
                  pl.BlockSpec((1, tn), lambda b, i, j: (0, j)),
                  pl.BlockSpec((1, tn), lambda b, i, j: (0, nj + j))],
        out_specs=pl.BlockSpec((None, tm, tn), lambda b, i, j: (b, i, j)),
        out_shape=jax.ShapeDtypeStruct((batch, seq, d), BF16),
        scratch_shapes=[pltpu.VMEM((tm, aw), BF16),
                        pltpu.VMEM((n_heads, tm, HEAD_DIM), F32), pltpu.VMEM((n_heads, tm, HEAD_DIM), F32),
                        pltpu.VMEM((tm, LANES), F32), pltpu.VMEM((tm, LANES), F32)],
        compiler_params=_params(("parallel", "parallel", "arbitrary")),
        name="branches",
    )(outs[0], outs[1], outs[2], lses[0], lses[1], lses[2], proj, a_four,
      w_attn, w_four, proj, proj, bias, bias)


def _outproj_kernel(a_ref, w_ref, x_ref, *refs, nj, final_norm):
    o_ref = refs[-1]
    j = pl.program_id(1)
    tn = w_ref.shape[1]
    for jj in range(nj):
        @pl.when(j == jj)
        def _(jj=jj):
            y = x_ref[...] + jnp.dot(a_ref[...], w_ref[...], preferred_element_type=F32)
            if not (final_norm and jj == nj - 1):
                o_ref[:, jj * tn:(jj + 1) * tn] = y
                return
            g = refs[0][...]
            done = jj * tn
            n = done + tn
            chunk = 128
            for c in range(o_ref.shape[0] // chunk):
                rows = slice(c * chunk, (c + 1) * chunk)
                v = o_ref[rows, :done]
                yr = y[rows, :]
                ss = jnp.sum(v * v, axis=-1, keepdims=True) + jnp.sum(yr * yr, axis=-1, keepdims=True)
                inv = lax.rsqrt(ss / n + NORM_EPS)
                o_ref[rows, :done] = v * inv * g[:, :done]
                o_ref[rows, done:] = yr * inv * g[:, done:]


def _outproj(a, w, x, final_gain=None, tm=512, tn=1024):
    m, k = a.shape
    n = w.shape[1]
    nj = n // tn
    operands = [a, w, x]
    in_specs = [pl.BlockSpec((tm, k), lambda i, j: (i, 0)),
                pl.BlockSpec((k, tn), lambda i, j: (0, j)),
                pl.BlockSpec((tm, tn), lambda i, j: (i, j))]
    if final_gain is not None:
        operands.append(final_gain.reshape(1, n).astype(F32))
        in_specs.append(pl.BlockSpec((1, n), lambda i, j: (0, 0)))
    return pl.pallas_call(
        functools.partial(_outproj_kernel, nj=nj, final_norm=final_gain is not None),
        grid=(m // tm, nj),
        in_specs=in_specs,
        out_specs=pl.BlockSpec((tm, n), lambda i, j: (i, 0)),
        out_shape=jax.ShapeDtypeStruct((m, n), F32),
        compiler_params=_params(("parallel", "arbitrary")),
        name="outproj",
    )(*operands)


def _rotary_tables(seq):
    half = HEAD_DIM // 2
    inv_freq = ROPE_THETA ** (-jnp.arange(half, dtype=F32) * (2.0 / HEAD_DIM))
    ang = jnp.arange(seq, dtype=jnp.int32).astype(F32)[:, None] * inv_freq[None, :]
    cos, sin = jnp.cos(ang), jnp.sin(ang)
    return jnp.concatenate([cos, cos], axis=-1), jnp.concatenate([-sin, sin], axis=-1)


def kernel(x, norm_gain, w_in, gate_bias, w_branch_attn, w_branch_fourier, w_out, final_norm_gain):
    batch, seq, d = x.shape
    depth = norm_gain.shape[0]
    attn_width = w_branch_attn.shape[1]
    four_width = w_branch_fourier.shape[1]
    n_heads = attn_width // HEAD_DIM
    tn = attn_width
    assert four_width == 2 * tn and d % tn == 0
    assert w_in.shape[2] == (3 * N_GROUPS + 1) * tn + 2 * four_width + 2 * d
    assert all(w // (2 * dil) == ATTN_RADIUS for w, dil in DILATED_GROUPS)
    dilations = tuple(dil for _, dil in DILATED_GROUPS)
    assert dilations[0] == 1 and FFT_P in dilations
    q0, k0, v0 = 0, N_GROUPS, 2 * N_GROUPS
    za = 3 * N_GROUPS
    u0 = za + 1
    zf0 = u0 + 2
    g0 = zf0 + 2
    n_gate_tiles = 2 * d // tn

    cos_tab, sin_tab = _rotary_tables(seq)
    for layer in range(depth):
        *hs, w = _rmsnorm(x, norm_gain[layer], w_in[layer], dilations[1:])
        outs, lses = [], []
        a_four = None
        proj = None
        late_w = [w_branch_attn[layer], w_branch_fourier[layer], w_out[layer]]
        assert len(late_w) == len(dilations)
        late_bf = []
        for g, dil in enumerate(dilations):
            tiles = [q0 + g, k0 + g, v0 + g]
            if dil == 1:
                tiles += [za, zf0, zf0 + 1] + [g0 + t for t in range(n_gate_tiles)]
            q_tile = 0
            if dil == FFT_P:
                tiles = [u0, u0 + 1] + tiles
                q_tile = 2
            pg = _inproj(hs[g].reshape(batch * seq, d), w, tiles, q_tile, cos_tab, sin_tab, seq, dil, tn=tn)
            cols = pg.shape[1]
            o_g, lse_g, w_bf = _banded_attention(pg.reshape(batch * dil, seq // dil, cols), q_tile, n_heads,
                                                 late_w[g])
            late_bf.append(w_bf)
            if dil == 1:
                proj = pg.reshape(batch, seq, cols)
                outs.append(o_g)
                lses.append(lse_g)
            else:
                outs.append(o_g.reshape(batch, dil, seq // dil, attn_width))
                lses.append(lse_g.reshape(batch, dil, seq // dil, LANES))
            if dil == FFT_P:
                a_four = _fourier_mix(pg.reshape(batch, dil, seq // dil, cols), 0, proj, 2,
                                      batch, seq, four_width)
        mixed = _branches(proj, outs, lses, a_four, late_bf[0], late_bf[1], gate_bias[layer],
                          3, 6 * tn, n_heads)
        x = _outproj(mixed.reshape(batch * seq, d), late_bf[2], x.reshape(batch * seq, d),
                     final_norm_gain if layer == depth - 1 else None).reshape(batch, seq, d)
    return x
```

```python
import functools

import numpy as np
import jax
import jax.numpy as jnp
from jax import lax
from jax.experimental import pallas as pl
from jax.experimental.pallas import tpu as pltpu

HEAD_DIM = 128
DILATED_GROUPS = ((128, 1), (512, 4), (2048, 16))
N_GROUPS = len(DILATED_GROUPS)
FOURIER_GROUPS = 8
ROPE_THETA = 10000.0
NORM_EPS = 1e-6
NEG_INF = -1e30
ATTN_RADIUS = 64
FFT_P = 16
FFT_F2_BLOCK = 16
LANES = 128
VMEM_LIMIT = 56 * 1024 * 1024

F32 = jnp.float32
BF16 = jnp.bfloat16


def _params(sem, vmem=VMEM_LIMIT):
    return pltpu.CompilerParams(dimension_semantics=sem, vmem_limit_bytes=vmem)


def _rmsnorm_kernel(x_ref, g_ref, w_ref, o_ref, *refs, dilations):
    od_refs, wb_ref, ybuf = refs[:-2], refs[-2], refs[-1]
    x = x_ref[...].astype(F32)
    ms = jnp.mean(x * x, axis=-1, keepdims=True)
    y = x * lax.rsqrt(ms + NORM_EPS) * g_ref[...]
    o_ref[...] = y.astype(o_ref.dtype)
    rows = y.shape[0]
    chunks = [slice(c * LANES, (c + 1) * LANES) for c in range(y.shape[1] // LANES)]
    for c, sl in enumerate(chunks):
        ybuf[c] = y[:, sl]
    for d, od_ref in zip(dilations, od_refs):
        for r in range(d):
            for c, sl in enumerate(chunks):
                od_ref[r, :, sl] = ybuf[c, pl.ds(r, rows // d, stride=d), :].astype(od_ref.dtype)
    wb_ref[...] = w_ref[...].astype(wb_ref.dtype)


def _rmsnorm(x, gain, w, dilations, tm=256):
    batch, seq, d = x.shape
    steps = batch * (seq // tm)
    wk, wn = w.shape
    assert wk % steps == 0
    wr = wk // steps
    out_specs = [pl.BlockSpec((None, tm, d), lambda b, i: (b, i, 0))]
    out_shape = [jax.ShapeDtypeStruct((batch, seq, d), BF16)]
    for dil in dilations:
        out_specs.append(pl.BlockSpec((None, dil, tm // dil, d), lambda b, i: (b, 0, i, 0)))
        out_shape.append(jax.ShapeDtypeStruct((batch, dil, seq // dil, d), BF16))
    w_index = lambda b, i: (b * (seq // tm) + i, 0)
    out_specs.append(pl.BlockSpec((wr, wn), w_index))
    out_shape.append(jax.ShapeDtypeStruct((wk, wn), BF16))
    return pl.pallas_call(
        functools.partial(_rmsnorm_kernel, dilations=tuple(dilations)),
        grid=(batch, seq // tm),
        in_specs=[pl.BlockSpec((None, tm, d), lambda b, i: (b, i, 0)),
                  pl.BlockSpec((1, d), lambda b, i: (0, 0)),
                  pl.BlockSpec((wr, wn), w_index)],
        out_specs=out_specs,
        out_shape=out_shape,
        scratch_shapes=[pltpu.VMEM((d // LANES, tm, LANES), F32)],
        compiler_params=_params(("parallel", "parallel")),
        name="rmsnorm",
    )(x, gain.reshape(1, d).astype(F32), w)


def _inproj_kernel(tiles_ref, h_ref, w_ref, cos0_ref, cos1_ref, sin0_ref, sin1_ref, o_ref, *, rot_lo):
    del tiles_ref
    j = pl.program_id(1)
    is_rot = (j >= rot_lo) & (j < rot_lo + 2)
    tm, tn = o_ref.shape
    half = tm // 2
    chunk = 2 * HEAD_DIM

    @pl.when(is_rot)
    def _():
        scale = jnp.where(j == rot_lo, HEAD_DIM ** -0.5, 1.0).astype(F32)
        tabs = [(cos0_ref[...] * scale, sin0_ref[...] * scale), (cos1_ref[...] * scale, sin1_ref[...] * scale)]
        for ck in range(tn // chunk):
            acc = jnp.dot(h_ref[...], w_ref[:, ck * chunk:(ck + 1) * chunk], preferred_element_type=F32)
            for a, (c, s) in enumerate(tabs):
                rows = slice(a * half, (a + 1) * half)
                for hd in range(chunk // HEAD_DIM):
                    xh = acc[rows, hd * HEAD_DIM:(hd + 1) * HEAD_DIM]
                    lo = ck * chunk + hd * HEAD_DIM
                    o_ref[rows, lo:lo + HEAD_DIM] = (
                        xh * c + pltpu.roll(xh, HEAD_DIM // 2, axis=1) * s).astype(o_ref.dtype)

    @pl.when(jnp.logical_not(is_rot))
    def _():
        o_ref[...] = jnp.dot(h_ref[...], w_ref[...], preferred_element_type=F32).astype(o_ref.dtype)


def _inproj(h, w, col_tiles, rot_lo, cos_tab, sin_tab, seq, dilation, tm=1024, tn=1024):
    m, k = h.shape
    sub = seq // dilation
    half = tm // 2
    assert sub % half == 0
    cos_v = cos_tab.reshape(sub, dilation * HEAD_DIM)
    sin_v = sin_tab.reshape(sub, dilation * HEAD_DIM)

    def tab_spec(a):
        def index(i, j, t):
            pos = (i * tm + a * half) % seq
            return ((pos % sub) // half, pos // sub)
        return pl.BlockSpec((half, HEAD_DIM), index)

    grid_spec = pltpu.PrefetchScalarGridSpec(
        num_scalar_prefetch=1,
        grid=(m // tm, len(col_tiles)),
        in_specs=[pl.BlockSpec((tm, k), lambda i, j, t: (i, 0)),
                  pl.BlockSpec((k, tn), lambda i, j, t: (0, t[j])),
                  tab_spec(0), tab_spec(1), tab_spec(0), tab_spec(1)],
        out_specs=pl.BlockSpec((tm, tn), lambda i, j, t: (i, j)),
    )
    return pl.pallas_call(
        functools.partial(_inproj_kernel, rot_lo=rot_lo),
        grid_spec=grid_spec,
        out_shape=jax.ShapeDtypeStruct((m, len(col_tiles) * tn), BF16),
        compiler_params=_params(("parallel", "arbitrary")),
        name="inproj",
    )(jnp.asarray(np.asarray(col_tiles, np.int32)), h, w, cos_v, cos_v, sin_v, sin_v)


def _attn_kernel(q_ref, kp_ref, kc_ref, kn_ref, vp_ref, vc_ref, vn_ref, w_ref, o_ref, lse_ref, wb_ref,
                 kbuf, vbuf, *, tq, sub, n_heads):
    wb_ref[...] = w_ref[...].astype(wb_ref.dtype)
    it = pl.program_id(1)
    r = ATTN_RADIUS
    qb = 2 * r
    kb = 4 * r
    kbuf[0:r, :] = kp_ref[...]
    kbuf[r:r + tq, :] = kc_ref[...]
    kbuf[r + tq:, :] = kn_ref[...]
    vbuf[0:r, :] = vp_ref[...]
    vbuf[r:r + tq, :] = vc_ref[...]
    vbuf[r + tq:, :] = vn_ref[...]

    def body(i, carry):
        r0 = pl.multiple_of(i * qb, qb)
        row = lax.broadcasted_iota(jnp.int32, (qb, kb), 0)
        col = lax.broadcasted_iota(jnp.int32, (qb, kb), 1)
        delta = col - row
        kpos = col + (it * tq + i * qb - r)
        valid = (delta >= 0) & (delta <= 2 * r) & (kpos >= 0) & (kpos < sub)
        lane = lax.broadcasted_iota(jnp.int32, (qb, LANES), 1)
        lse_tile = jnp.zeros((qb, LANES), F32)
        for h in range(n_heads):
            sl = slice(h * HEAD_DIM, (h + 1) * HEAD_DIM)
            q = q_ref[pl.ds(r0, qb), sl]
            k = kbuf[pl.ds(r0, kb), sl]
            v = vbuf[pl.ds(r0, kb), sl]
            s = lax.dot_general(q, k, (((1,), (1,)), ((), ())), preferred_element_type=F32)
            s = jnp.where(valid, s, NEG_INF)
            m = jnp.max(s, axis=-1, keepdims=True)
            p = jnp.exp(s - m)
            l = jnp.sum(p, axis=-1, keepdims=True)
            o = jnp.dot(p.astype(BF16), v, preferred_element_type=F32)
            o_ref[pl.ds(r0, qb), sl] = (o / l).astype(o_ref.dtype)
            lse_tile = jnp.where(lane == h, m + jnp.log(l), lse_tile)
        lse_ref[pl.ds(r0, qb), :] = lse_tile
        return carry

    lax.fori_loop(0, tq // qb, body, 0, unroll=True)


def _banded_attention(qkv, q_tile, n_heads, w, tq=1024):
    n, sub, _ = qkv.shape
    width = n_heads * HEAD_DIM
    tq = min(tq, sub)
    r = ATTN_RADIUS
    per = tq // r
    nt = sub // tq
    wk, wn = w.shape
    assert wk % (n * nt) == 0
    w_spec = pl.BlockSpec((wk // (n * nt), wn), lambda b, t: (b * nt + t, 0))

    def cur(off):
        return pl.BlockSpec((None, tq, width), lambda b, t: (b, t, q_tile + off))

    def prev(off):
        return pl.BlockSpec((None, r, width), lambda b, t: (b, jnp.maximum(t * per - 1, 0), q_tile + off))

    def nxt(off):
        return pl.BlockSpec((None, r, width),
                            lambda b, t: (b, jnp.minimum((t + 1) * per, sub // r - 1), q_tile + off))

    return pl.pallas_call(
        functools.partial(_attn_kernel, tq=tq, sub=sub, n_heads=n_heads),
        grid=(n, nt),
        in_specs=[cur(0), prev(1), cur(1), nxt(1), prev(2), cur(2), nxt(2), w_spec],
        out_specs=[pl.BlockSpec((None, tq, width), lambda b, t: (b, t, 0)),
                   pl.BlockSpec((None, tq, LANES), lambda b, t: (b, t, 0)),
                   w_spec],
        out_shape=[jax.ShapeDtypeStruct((n, sub, width), BF16),
                   jax.ShapeDtypeStruct((n, sub, LANES), F32),
                   jax.ShapeDtypeStruct((wk, wn), BF16)],
        scratch_shapes=[pltpu.VMEM((tq + 2 * r, width), BF16),
                        pltpu.VMEM((tq + 2 * r, width), BF16)],
        compiler_params=_params(("parallel", "parallel")),
        name=f"attn_sub{sub}",
    )(qkv, qkv, qkv, qkv, qkv, qkv, qkv, w)


def _dft_tables(seq, gdim):
    p, fb = FFT_P, FFT_F2_BLOCK
    q = seq // p
    c = np.arange(gdim, dtype=np.float64)
    ang = 2 * np.pi * np.outer(c, c) / gdim
    wc = np.concatenate([np.cos(ang), np.sin(ang)], axis=1) / np.sqrt(gdim)
    aq = 2 * np.pi * (np.outer(np.arange(q), np.arange(q)) % q) / q
    cq, sq = np.cos(aq) / np.sqrt(q), np.sin(aq) / np.sqrt(q)
    w1 = np.stack([cq, -(sq + cq), sq - cq])
    f = (q * np.arange(p)[None, :, None, None]
         + fb * np.arange(q // fb)[:, None, None, None] + np.arange(fb)[None, None, :, None])
    phi = 2 * np.pi * ((f * np.arange(p)[None, None, None, :]) % seq) / seq
    eye = np.eye(fb)
    parts = [np.einsum('afjs,jk->afjsk', t / np.sqrt(p), eye) for t in (np.cos(phi), np.sin(phi))]
    big = np.stack(parts, axis=3).reshape(q // fb, p * fb, 2 * p * fb)
    as_bf16 = lambda a: jnp.asarray(a.astype(np.float32)).astype(BF16)
    return as_bf16(wc), as_bf16(w1), as_bf16(big)


def _fft1_kernel(u_ref, wc_ref, w1_ref, h_ref, *, gdim):
    for g in range(u_ref.shape[1] // gdim):
        sl = slice(g * gdim, (g + 1) * gdim)
        ab = jnp.dot(u_ref[:, sl], wc_ref[...], preferred_element_type=F32)
        a, b = ab[:, :gdim], ab[:, gdim:]
        k1 = jnp.dot(w1_ref[0], (a - b).astype(BF16), preferred_element_type=F32)
        k2 = jnp.dot(w1_ref[1], a.astype(BF16), preferred_element_type=F32)
        k3 = jnp.dot(w1_ref[2], b.astype(BF16), preferred_element_type=F32)
        h_ref[0, :, sl] = (k1 - k3).astype(h_ref.dtype)
        h_ref[1, :, sl] = (k1 + k2).astype(h_ref.dtype)


def _sigmoid(z):
    return 0.5 * jnp.tanh(0.5 * z) + 0.5


def _silu(z):
    h = 0.5 * z
    return h + h * jnp.tanh(h)


def _fft2_kernel(h_ref, big_ref, z_ref, o_ref):
    parts, p, fb, width = h_ref.shape
    rhs = h_ref[...].reshape(parts * p * fb, width)
    y = jnp.dot(big_ref[...], rhs, preferred_element_type=F32)
    o_ref[...] = (y.reshape(p, fb, width) * _silu(z_ref[...].astype(F32))).astype(o_ref.dtype)


def _fourier_mix(u_cm, u_tile, proj, z_tile, batch, seq, width):
    gdim = width // FOURIER_GROUPS
    p, fb = FFT_P, FFT_F2_BLOCK
    q = seq // p
    wc, w1, big = _dft_tables(seq, gdim)
    h = pl.pallas_call(
        functools.partial(_fft1_kernel, gdim=gdim),
        grid=(batch, p),
        in_specs=[pl.BlockSpec((None, None, q, width), lambda b, s: (b, s, 0, u_tile)),
                  pl.BlockSpec(wc.shape, lambda b, s: (0, 0)),
                  pl.BlockSpec(w1.shape, lambda b, s: (0, 0, 0))],
        out_specs=pl.BlockSpec((None, 2, None, q, width), lambda b, s: (b, 0, s, 0, 0)),
        out_shape=jax.ShapeDtypeStruct((batch, 2, p, q, width), BF16),
        compiler_params=_params(("parallel", "parallel")),
        name="fft_seq1",
    )(u_cm, wc, w1)
    y = pl.pallas_call(
        _fft2_kernel,
        grid=(batch, q // fb),
        in_specs=[pl.BlockSpec((None, 2, p, fb, width), lambda b, a: (b, 0, 0, a, 0)),
                  pl.BlockSpec((None, p * fb, 2 * p * fb), lambda b, a: (a, 0, 0)),
                  pl.BlockSpec((None, p, None, fb, width), lambda b, a: (b, 0, a, 0, z_tile))],
        out_specs=pl.BlockSpec((None, p, None, fb, width), lambda b, a: (b, 0, a, 0, 0)),
        out_shape=jax.ShapeDtypeStruct((batch, p, q // fb, fb, width), BF16),
        compiler_params=_params(("parallel", "parallel")),
        name="fft_seq2",
    )(h, big, proj.reshape(batch, p, q // fb, fb, proj.shape[-1]))
    return y.reshape(batch, seq, width)


PERM_T = 16


def _deinterleave_matrix(d):
    p = np.zeros((d * PERM_T, d * PERM_T), np.float32)
    for r in range(d):
        for t in range(PERM_T):
            p[t * d + r, r * PERM_T + t] = 1.0
    return jnp.asarray(p).astype(BF16)


def _branch_kernel(o1_ref, o2_ref, o3_ref, l1_ref, l2_ref, l3_ref, za_ref, af_ref, p2_ref, p3_ref,
                   wa_ref, wf_ref, ga_ref, gf_ref, ba_ref, bf_ref, out_ref,
                   a_attn, on2, on3, ln2, ln3, *, n_heads):
    def emit_tile():
        br_f = jnp.dot(af_ref[...], wf_ref[...], preferred_element_type=F32)
        br_a = jnp.dot(a_attn[...], wa_ref[...], preferred_element_type=F32)
        g_a = _sigmoid(ga_ref[...].astype(F32) + ba_ref[...])
        g_f = _sigmoid(gf_ref[...].astype(F32) + bf_ref[...])
        out_ref[...] = (g_a * br_a + g_f * br_f).astype(out_ref.dtype)

    first = pl.program_id(2) == 0

    @pl.when(first)
    def _():
        rows = ln2.shape[0]
        for src, perm_ref, dst in ((o2_ref, p2_ref, on2), (o3_ref, p3_ref, on3)):
            d = src.shape[0]
            blk = d * PERM_T
            for b in range(rows // blk):
                stacked = jnp.concatenate(
                    [src[r, b * PERM_T:(b + 1) * PERM_T, :] for r in range(d)], axis=0)
                dst[b * blk:(b + 1) * blk, :] = jnp.dot(perm_ref[...], stacked, preferred_element_type=F32)
        for src, dst in ((l2_ref, ln2), (l3_ref, ln3)):
            d = src.shape[0]
            for r in range(d):
                dst[pl.ds(r, rows // d, stride=d), :] = src[r]
        l1, l2, l3 = l1_ref[...], ln2[...], ln3[...]
        m = jnp.maximum(jnp.maximum(l1, l2), l3)
        e1, e2, e3 = jnp.exp(l1 - m), jnp.exp(l2 - m), jnp.exp(l3 - m)
        inv = 1.0 / (e1 + e2 + e3)
        al1, al2, al3 = e1 * inv, e2 * inv, e3 * inv
        for h in range(n_heads):
            sl = slice(h * HEAD_DIM, (h + 1) * HEAD_DIM)
            o = (al1[:, h:h + 1] * o1_ref[:, sl].astype(F32)
                 + al2[:, h:h + 1] * on2[:, sl]
                 + al3[:, h:h + 1] * on3[:, sl])
            a_attn[:, sl] = (o * _silu(za_ref[:, sl].astype(F32))).astype(a_attn.dtype)
        emit_tile()

    pl.when(jnp.logical_not(first))(emit_tile)


def _branches(proj, outs, lses, a_four, w_attn, w_four, gate_bias, za_tile, gate_off,
              n_heads, tm=1024, tn=512):
    batch, seq, _ = proj.shape
    aw, fw = w_attn.shape[0], w_four.shape[0]
    d = w_attn.shape[1]
    assert gate_off % tn == 0
    gate_tile = gate_off // tn
    nj = d // tn
    bias = gate_bias.reshape(1, 2 * d).astype(F32)

    def nat(width, tile=0):
        return pl.BlockSpec((None, tm, width), lambda b, i, j: (b, i, tile))

    def cm(arr):
        dil = arr.shape[1]
        return pl.BlockSpec((None, dil, tm // dil, arr.shape[3]), lambda b, i, j: (b, 0, i, 0))

    perms = [_deinterleave_matrix(o.shape[1]) for o in outs[1:]]
    whole = lambda arr: pl.BlockSpec(arr.shape, lambda b, i, j: (0, 0))

    return pl.pallas_call(
        functools.partial(_branch_kernel, n_heads=n_heads),
        grid=(batch, seq // tm, nj),
        in_specs=[nat(aw), cm(outs[1]), cm(outs[2]), nat(LANES), cm(lses[1]), cm(lses[2]),
                  nat(aw, za_tile), nat(fw), whole(perms[0]), whole(perms[1]),
                  pl.BlockSpec((aw, tn), lambda b, i, j: (0, j)),
                  pl.BlockSpec((fw, tn), lambda b, i, j: (0, j)),
                  pl.BlockSpec((None, tm, tn), lambda b, i, j: (b, i, gate_tile + j)),
                  pl.BlockSpec((None, tm, tn), lambda b, i, j: (b, i, gate_tile + nj + j)),
                  pl.BlockSpec((1, tn), lambda b, i, j: (0, j)),
                  pl.BlockSpec((1, tn), lambda b, i, j: (0, nj + j))],
        out_specs=pl.BlockSpec((None, tm, tn), lambda b, i, j: (b, i, j)),
        out_shape=jax.ShapeDtypeStruct((batch, seq, d), BF16),
        scratch_shapes=[pltpu.VMEM((tm, aw), BF16),
                        pltpu.VMEM((tm, aw), F32), pltpu.VMEM((tm, aw), F32),
                        pltpu.VMEM((tm, LANES), F32), pltpu.VMEM((tm, LANES), F32)],
        compiler_params=_params(("parallel", "parallel", "arbitrary")),
        name="branches",
    )(outs[0], outs[1], outs[2], lses[0], lses[1], lses[2], proj, a_four, perms[0], perms[1],
      w_attn, w_four, proj, proj, bias, bias)


def _outproj_kernel(a_ref, w_ref, x_ref, *refs, nj, final_norm):
    o_ref = refs[-1]
    j = pl.program_id(1)
    tn = w_ref.shape[1]
    for jj in range(nj):
        @pl.when(j == jj)
        def _(jj=jj):
            y = x_ref[...] + jnp.dot(a_ref[...], w_ref[...], preferred_element_type=F32)
            if not (final_norm and jj == nj - 1):
                o_ref[:, jj * tn:(jj + 1) * tn] = y
                return
            g = refs[0][...]
            done = jj * tn
            n = done + tn
            chunk = 128
            for c in range(o_ref.shape[0] // chunk):
                rows = slice(c * chunk, (c + 1) * chunk)
                v = o_ref[rows, :done]
                yr = y[rows, :]
                ss = jnp.sum(v * v, axis=-1, keepdims=True) + jnp.sum(yr * yr, axis=-1, keepdims=True)
                inv = lax.rsqrt(ss / n + NORM_EPS)
                o_ref[rows, :done] = v * inv * g[:, :done]
                o_ref[rows, done:] = yr * inv * g[:, done:]


def _outproj(a, w, x, final_gain=None, tm=512, tn=1024):
    m, k = a.shape
    n = w.shape[1]
    nj = n // tn
    operands = [a, w, x]
    in_specs = [pl.BlockSpec((tm, k), lambda i, j: (i, 0)),
                pl.BlockSpec((k, tn), lambda i, j: (0, j)),
                pl.BlockSpec((tm, tn), lambda i, j: (i, j))]
    if final_gain is not None:
        operands.append(final_gain.reshape(1, n).astype(F32))
        in_specs.append(pl.BlockSpec((1, n), lambda i, j: (0, 0)))
    return pl.pallas_call(
        functools.partial(_outproj_kernel, nj=nj, final_norm=final_gain is not None),
        grid=(m // tm, nj),
        in_specs=in_specs,
        out_specs=pl.BlockSpec((tm, n), lambda i, j: (i, 0)),
        out_shape=jax.ShapeDtypeStruct((m, n), F32),
        compiler_params=_params(("parallel", "arbitrary")),
        name="outproj",
    )(*operands)


def _rotary_tables(seq):
    half = HEAD_DIM // 2
    inv_freq = ROPE_THETA ** (-jnp.arange(half, dtype=F32) * (2.0 / HEAD_DIM))
    ang = jnp.arange(seq, dtype=jnp.int32).astype(F32)[:, None] * inv_freq[None, :]
    cos, sin = jnp.cos(ang), jnp.sin(ang)
    return jnp.concatenate([cos, cos], axis=-1), jnp.concatenate([-sin, sin], axis=-1)


def kernel(x, norm_gain, w_in, gate_bias, w_branch_attn, w_branch_fourier, w_out, final_norm_gain):
    batch, seq, d = x.shape
    depth = norm_gain.shape[0]
    attn_width = w_branch_attn.shape[1]
    four_width = w_branch_fourier.shape[1]
    n_heads = attn_width // HEAD_DIM
    tn = attn_width
    assert four_width == 2 * tn and d % tn == 0
    assert w_in.shape[2] == (3 * N_GROUPS + 1) * tn + 2 * four_width + 2 * d
    assert all(w // (2 * dil) == ATTN_RADIUS for w, dil in DILATED_GROUPS)
    dilations = tuple(dil for _, dil in DILATED_GROUPS)
    assert dilations[0] == 1 and FFT_P in dilations
    q0, k0, v0 = 0, N_GROUPS, 2 * N_GROUPS
    za = 3 * N_GROUPS
    u0 = za + 1
    zf0 = u0 + 2
    g0 = zf0 + 2
    n_gate_tiles = 2 * d // tn

    cos_tab, sin_tab = _rotary_tables(seq)
    for layer in range(depth):
        *hs, w = _rmsnorm(x, norm_gain[layer], w_in[layer], dilations[1:])
        outs, lses = [], []
        a_four = None
        proj = None
        late_w = [w_branch_attn[layer], w_branch_fourier[layer], w_out[layer]]
        assert len(late_w) == len(dilations)
        late_bf = []
        for g, dil in enumerate(dilations):
            tiles = [q0 + g, k0 + g, v0 + g]
            if dil == 1:
                tiles += [za, zf0, zf0 + 1] + [g0 + t for t in range(n_gate_tiles)]
            q_tile = 0
            if dil == FFT_P:
                tiles = [u0, u0 + 1] + tiles
                q_tile = 2
            pg = _inproj(hs[g].reshape(batch * seq, d), w, tiles, q_tile, cos_tab, sin_tab, seq, dil, tn=tn)
            cols = pg.shape[1]
            o_g, lse_g, w_bf = _banded_attention(pg.reshape(batch * dil, seq // dil, cols), q_tile, n_heads,
                                                 late_w[g])
            late_bf.append(w_bf)
            if dil == 1:
                proj = pg.reshape(batch, seq, cols)
                outs.append(o_g)
                lses.append(lse_g)
            else:
                outs.append(o_g.reshape(batch, dil, seq // dil, attn_width))
                lses.append(lse_g.reshape(batch, dil, seq // dil, LANES))
            if dil == FFT_P:
                a_four = _fourier_mix(pg.reshape(batch, dil, seq // dil, cols), 0, proj, 2,
                                      batch, seq, four_width)
        mixed = _branches(proj, outs, lses, a_four, late_bf[0], late_bf[1], gate_bias[layer],
                          3, 6 * tn, n_heads)
        x = _outproj(mixed.reshape(batch * seq, d), late_bf[2], x.reshape(batch * seq, d),
                     final_norm_gain if layer == depth - 1 else None).reshape(batch, seq, d)
    return x
```

```python
import functools

import numpy as np
import jax
import jax.numpy as jnp
from jax import lax
from jax.experimental import pallas as pl
from jax.experimental.pallas import tpu as pltpu

HEAD_DIM = 128
DILATED_GROUPS = ((128, 1), (512, 4), (2048, 16))
N_GROUPS = len(DILATED_GROUPS)
FOURIER_GROUPS = 8
ROPE_THETA = 10000.0
NORM_EPS = 1e-6
NEG_INF = -1e30
ATTN_RADIUS = 64
FFT_P = 16
FFT_F2_BLOCK = 16
LANES = 128
VMEM_LIMIT = 56 * 1024 * 1024

F32 = jnp.float32
BF16 = jnp.bfloat16


def _params(sem, vmem=VMEM_LIMIT):
    return pltpu.CompilerParams(dimension_semantics=sem, vmem_limit_bytes=vmem)


def _rmsnorm_kernel(x_ref, g_ref, w_ref, o_ref, *refs, dilations):
    od_refs, wb_ref, ybuf = refs[:-2], refs[-2], refs[-1]
    x = x_ref[...].astype(F32)
    ms = jnp.mean(x * x, axis=-1, keepdims=True)
    y = x * lax.rsqrt(ms + NORM_EPS) * g_ref[...]
    o_ref[...] = y.astype(o_ref.dtype)
    rows = y.shape[0]
    chunks = [slice(c * LANES, (c + 1) * LANES) for c in range(y.shape[1] // LANES)]
    for c, sl in enumerate(chunks):
        ybuf[c] = y[:, sl]
    for d, od_ref in zip(dilations, od_refs):
        for r in range(d):
            for c, sl in enumerate(chunks):
                od_ref[r, :, sl] = ybuf[c, pl.ds(r, rows // d, stride=d), :].astype(od_ref.dtype)
    wb_ref[...] = w_ref[...].astype(wb_ref.dtype)


def _rmsnorm(x, gain, w, dilations, tm=256):
    batch, seq, d = x.shape
    steps = batch * (seq // tm)
    wk, wn = w.shape
    assert wk % steps == 0
    wr = wk // steps
    out_specs = [pl.BlockSpec((None, tm, d), lambda b, i: (b, i, 0))]
    out_shape = [jax.ShapeDtypeStruct((batch, seq, d), BF16)]
    for dil in dilations:
        out_specs.append(pl.BlockSpec((None, dil, tm // dil, d), lambda b, i: (b, 0, i, 0)))
        out_shape.append(jax.ShapeDtypeStruct((batch, dil, seq // dil, d), BF16))
    w_index = lambda b, i: (b * (seq // tm) + i, 0)
    out_specs.append(pl.BlockSpec((wr, wn), w_index))
    out_shape.append(jax.ShapeDtypeStruct((wk, wn), BF16))
    return pl.pallas_call(
        functools.partial(_rmsnorm_kernel, dilations=tuple(dilations)),
        grid=(batch, seq // tm),
        in_specs=[pl.BlockSpec((None, tm, d), lambda b, i: (b, i, 0)),
                  pl.BlockSpec((1, d), lambda b, i: (0, 0)),
                  pl.BlockSpec((wr, wn), w_index)],
        out_specs=out_specs,
        out_shape=out_shape,
        scratch_shapes=[pltpu.VMEM((d // LANES, tm, LANES), F32)],
        compiler_params=_params(("parallel", "parallel")),
        name="rmsnorm",
    )(x, gain.reshape(1, d).astype(F32), w)


def _inproj_kernel(tiles_ref, h_ref, w_ref, cos0_ref, cos1_ref, sin0_ref, sin1_ref, o_ref, *, rot_lo):
    del tiles_ref
    j = pl.program_id(1)
    is_rot = (j >= rot_lo) & (j < rot_lo + 2)
    tm, tn = o_ref.shape
    half = tm // 2
    chunk = 2 * HEAD_DIM

    @pl.when(is_rot)
    def _():
        scale = jnp.where(j == rot_lo, HEAD_DIM ** -0.5, 1.0).astype(F32)
        tabs = [(cos0_ref[...] * scale, sin0_ref[...] * scale), (cos1_ref[...] * scale, sin1_ref[...] * scale)]
        for ck in range(tn // chunk):
            acc = jnp.dot(h_ref[...], w_ref[:, ck * chunk:(ck + 1) * chunk], preferred_element_type=F32)
            for a, (c, s) in enumerate(tabs):
                rows = slice(a * half, (a + 1) * half)
                for hd in range(chunk // HEAD_DIM):
                    xh = acc[rows, hd * HEAD_DIM:(hd + 1) * HEAD_DIM]
                    lo = ck * chunk + hd * HEAD_DIM
                    o_ref[rows, lo:lo + HEAD_DIM] = (
                        xh * c + pltpu.roll(xh, HEAD_DIM // 2, axis=1) * s).astype(o_ref.dtype)

    @pl.when(jnp.logical_not(is_rot))
    def _():
        o_ref[...] = jnp.dot(h_ref[...], w_ref[...], preferred_element_type=F32).astype(o_ref.dtype)


def _inproj(h, w, col_tiles, rot_lo, cos_tab, sin_tab, seq, dilation, tm=1024, tn=1024):
    m, k = h.shape
    sub = seq // dilation
    half = tm // 2
    assert sub % half == 0
    cos_v = cos_tab.reshape(sub, dilation * HEAD_DIM)
    sin_v = sin_tab.reshape(sub, dilation * HEAD_DIM)

    def tab_spec(a):
        def index(i, j, t):
            pos = (i * tm + a * half) % seq
            return ((pos % sub) // half, pos // sub)
        return pl.BlockSpec((half, HEAD_DIM), index)

    grid_spec = pltpu.PrefetchScalarGridSpec(
        num_scalar_prefetch=1,
        grid=(m // tm, len(col_tiles)),
        in_specs=[pl.BlockSpec((tm, k), lambda i, j, t: (i, 0)),
                  pl.BlockSpec((k, tn), lambda i, j, t: (0, t[j])),
                  tab_spec(0), tab_spec(1), tab_spec(0), tab_spec(1)],
        out_specs=pl.BlockSpec((tm, tn), lambda i, j, t: (i, j)),
    )
    return pl.pallas_call(
        functools.partial(_inproj_kernel, rot_lo=rot_lo),
        grid_spec=grid_spec,
        out_shape=jax.ShapeDtypeStruct((m, len(col_tiles) * tn), BF16),
        compiler_params=_params(("parallel", "arbitrary")),
        name="inproj",
    )(jnp.asarray(np.asarray(col_tiles, np.int32)), h, w, cos_v, cos_v, sin_v, sin_v)


def _attn_kernel(q_ref, kp_ref, kc_ref, kn_ref, vp_ref, vc_ref, vn_ref, *refs, tq, sub, n_heads, rounds_w):
    if rounds_w:
        w_ref, o_ref, lse_ref, wb_ref, kbuf, vbuf = refs
        wb_ref[...] = w_ref[...].astype(wb_ref.dtype)
    else:
        o_ref, lse_ref, kbuf, vbuf = refs
    it = pl.program_id(1)
    r = ATTN_RADIUS
    qb = 2 * r
    kb = 4 * r
    kbuf[0:r, :] = kp_ref[...]
    kbuf[r:r + tq, :] = kc_ref[...]
    kbuf[r + tq:, :] = kn_ref[...]
    vbuf[0:r, :] = vp_ref[...]
    vbuf[r:r + tq, :] = vc_ref[...]
    vbuf[r + tq:, :] = vn_ref[...]

    def body(i, carry):
        r0 = pl.multiple_of(i * qb, qb)
        row = lax.broadcasted_iota(jnp.int32, (qb, kb), 0)
        col = lax.broadcasted_iota(jnp.int32, (qb, kb), 1)
        delta = col - row
        kpos = col + (it * tq + i * qb - r)
        valid = (delta >= 0) & (delta <= 2 * r) & (kpos >= 0) & (kpos < sub)
        lane = lax.broadcasted_iota(jnp.int32, (qb, LANES), 1)
        lse_tile = jnp.zeros((qb, LANES), F32)
        for h in range(n_heads):
            sl = slice(h * HEAD_DIM, (h + 1) * HEAD_DIM)
            q = q_ref[pl.ds(r0, qb), sl]
            k = kbuf[pl.ds(r0, kb), sl]
            v = vbuf[pl.ds(r0, kb), sl]
            s = lax.dot_general(q, k, (((1,), (1,)), ((), ())), preferred_element_type=F32)
            s = jnp.where(valid, s, NEG_INF)
            m = jnp.max(s, axis=-1, keepdims=True)
            p = jnp.exp(s - m)
            l = jnp.sum(p, axis=-1, keepdims=True)
            o = jnp.dot(p.astype(BF16), v, preferred_element_type=F32)
            o_ref[pl.ds(r0, qb), sl] = (o / l).astype(o_ref.dtype)
            lse_tile = jnp.where(lane == h, m + jnp.log(l), lse_tile)
        lse_ref[pl.ds(r0, qb), :] = lse_tile
        return carry

    lax.fori_loop(0, tq // qb, body, 0, unroll=True)


def _round_spec(w, steps, index):
    wk, wn = w.shape
    assert wk % steps == 0
    return pl.BlockSpec((wk // steps, wn), index)


def _banded_attention(qkv, q_tile, n_heads, w=None, tq=1024):
    n, sub, _ = qkv.shape
    width = n_heads * HEAD_DIM
    tq = min(tq, sub)
    r = ATTN_RADIUS
    per = tq // r
    nt = sub // tq
    extra_in, extra_spec, extra_shape = [], [], []
    if w is not None:
        extra_in = [w]
        extra_spec = [_round_spec(w, n * nt, lambda b, t: (b * nt + t, 0))]
        extra_shape = [jax.ShapeDtypeStruct(w.shape, BF16)]

    def cur(off):
        return pl.BlockSpec((None, tq, width), lambda b, t: (b, t, q_tile + off))

    def prev(off):
        return pl.BlockSpec((None, r, width), lambda b, t: (b, jnp.maximum(t * per - 1, 0), q_tile + off))

    def nxt(off):
        return pl.BlockSpec((None, r, width),
                            lambda b, t: (b, jnp.minimum((t + 1) * per, sub // r - 1), q_tile + off))

    return pl.pallas_call(
        functools.partial(_attn_kernel, tq=tq, sub=sub, n_heads=n_heads, rounds_w=w is not None),
        grid=(n, nt),
        in_specs=[cur(0), prev(1), cur(1), nxt(1), prev(2), cur(2), nxt(2)] + extra_spec,
        out_specs=[pl.BlockSpec((None, tq, width), lambda b, t: (b, t, 0)),
                   pl.BlockSpec((None, tq, LANES), lambda b, t: (b, t, 0))] + extra_spec,
        out_shape=[jax.ShapeDtypeStruct((n, sub, width), BF16),
                   jax.ShapeDtypeStruct((n, sub, LANES), F32)] + extra_shape,
        scratch_shapes=[pltpu.VMEM((tq + 2 * r, width), BF16),
                        pltpu.VMEM((tq + 2 * r, width), BF16)],
        compiler_params=_params(("parallel", "parallel")),
        name=f"attn_sub{sub}",
    )(qkv, qkv, qkv, qkv, qkv, qkv, qkv, *extra_in)


def _dft_tables(seq, gdim):
    p, fb = FFT_P, FFT_F2_BLOCK
    q = seq // p
    c = np.arange(gdim, dtype=np.float64)
    ang = 2 * np.pi * np.outer(c, c) / gdim
    wc = np.concatenate([np.cos(ang), np.sin(ang)], axis=1) / np.sqrt(gdim)
    aq = 2 * np.pi * (np.outer(np.arange(q), np.arange(q)) % q) / q
    cq, sq = np.cos(aq) / np.sqrt(q), np.sin(aq) / np.sqrt(q)
    w1 = np.stack([cq, -(sq + cq), sq - cq])
    f = (q * np.arange(p)[None, :, None, None]
         + fb * np.arange(q // fb)[:, None, None, None] + np.arange(fb)[None, None, :, None])
    phi = 2 * np.pi * ((f * np.arange(p)[None, None, None, :]) % seq) / seq
    eye = np.eye(fb)
    parts = [np.einsum('afjs,jk->afjsk', t / np.sqrt(p), eye) for t in (np.cos(phi), np.sin(phi))]
    big = np.stack(parts, axis=3).reshape(q // fb, p * fb, 2 * p * fb)
    as_bf16 = lambda a: jnp.asarray(a.astype(np.float32)).astype(BF16)
    return as_bf16(wc), as_bf16(w1), as_bf16(big)


def _fft1_kernel(u_ref, wc_ref, w1_ref, w_ref, h_ref, wb_ref, *, gdim):
    wb_ref[...] = w_ref[...].astype(wb_ref.dtype)
    for g in range(u_ref.shape[1] // gdim):
        sl = slice(g * gdim, (g + 1) * gdim)
        ab = jnp.dot(u_ref[:, sl], wc_ref[...], preferred_element_type=F32)
        a, b = ab[:, :gdim], ab[:, gdim:]
        k1 = jnp.dot(w1_ref[0], (a - b).astype(BF16), preferred_element_type=F32)
        k2 = jnp.dot(w1_ref[1], a.astype(BF16), preferred_element_type=F32)
        k3 = jnp.dot(w1_ref[2], b.astype(BF16), preferred_element_type=F32)
        h_ref[0, :, sl] = (k1 - k3).astype(h_ref.dtype)
        h_ref[1, :, sl] = (k1 + k2).astype(h_ref.dtype)


def _sigmoid(z):
    return 0.5 * jnp.tanh(0.5 * z) + 0.5


def _silu(z):
    h = 0.5 * z
    return h + h * jnp.tanh(h)


def _fft2_kernel(h_ref, big_ref, z_ref, o_ref):
    parts, p, fb, width = h_ref.shape
    rhs = h_ref[...].reshape(parts * p * fb, width)
    y = jnp.dot(big_ref[...], rhs, preferred_element_type=F32)
    o_ref[...] = (y.reshape(p, fb, width) * _silu(z_ref[...].astype(F32))).astype(o_ref.dtype)


def _fourier_mix(u_cm, u_tile, proj, z_tile, batch, seq, width, w):
    gdim = width // FOURIER_GROUPS
    p, fb = FFT_P, FFT_F2_BLOCK
    q = seq // p
    wc, w1, big = _dft_tables(seq, gdim)
    w_spec = _round_spec(w, batch * p, lambda b, s: (b * p + s, 0))
    h, w_bf = pl.pallas_call(
        functools.partial(_fft1_kernel, gdim=gdim),
        grid=(batch, p),
        in_specs=[pl.BlockSpec((None, None, q, width), lambda b, s: (b, s, 0, u_tile)),
                  pl.BlockSpec(wc.shape, lambda b, s: (0, 0)),
                  pl.BlockSpec(w1.shape, lambda b, s: (0, 0, 0)),
                  w_spec],
        out_specs=[pl.BlockSpec((None, 2, None, q, width), lambda b, s: (b, 0, s, 0, 0)), w_spec],
        out_shape=[jax.ShapeDtypeStruct((batch, 2, p, q, width), BF16),
                   jax.ShapeDtypeStruct(w.shape, BF16)],
        compiler_params=_params(("parallel", "parallel")),
        name="fft_seq1",
    )(u_cm, wc, w1, w)
    y = pl.pallas_call(
        _fft2_kernel,
        grid=(batch, q // fb),
        in_specs=[pl.BlockSpec((None, 2, p, fb, width), lambda b, a: (b, 0, 0, a, 0)),
                  pl.BlockSpec((None, p * fb, 2 * p * fb), lambda b, a: (a, 0, 0)),
                  pl.BlockSpec((None, p, None, fb, width), lambda b, a: (b, 0, a, 0, z_tile))],
        out_specs=pl.BlockSpec((None, p, None, fb, width), lambda b, a: (b, 0, a, 0, 0)),
        out_shape=jax.ShapeDtypeStruct((batch, p, q // fb, fb, width), BF16),
        compiler_params=_params(("parallel", "parallel")),
        name="fft_seq2",
    )(h, big, proj.reshape(batch, p, q // fb, fb, proj.shape[-1]))
    return y.reshape(batch, seq, width), w_bf


PERM_T = 16


def _deinterleave_matrix(d):
    p = np.zeros((d * PERM_T, d * PERM_T), np.float32)
    for r in range(d):
        for t in range(PERM_T):
            p[t * d + r, r * PERM_T + t] = 1.0
    return jnp.asarray(p).astype(BF16)


def _branch_kernel(o1_ref, o2_ref, o3_ref, l1_ref, l2_ref, l3_ref, za_ref, af_ref, p2_ref, p3_ref,
                   wa_ref, wf_ref, ga_ref, gf_ref, ba_ref, bf_ref, out_ref,
                   a_attn, on2, on3, ln2, ln3, *, n_heads):
    def emit_tile():
        br_f = jnp.dot(af_ref[...], wf_ref[...], preferred_element_type=F32)
        br_a = jnp.dot(a_attn[...], wa_ref[...], preferred_element_type=F32)
        g_a = _sigmoid(ga_ref[...].astype(F32) + ba_ref[...])
        g_f = _sigmoid(gf_ref[...].astype(F32) + bf_ref[...])
        out_ref[...] = (g_a * br_a + g_f * br_f).astype(out_ref.dtype)

    first = pl.program_id(2) == 0

    @pl.when(first)
    def _():
        rows = ln2.shape[0]
        for src, perm_ref, dst in ((o2_ref, p2_ref, on2), (o3_ref, p3_ref, on3)):
            d = src.shape[0]
            blk = d * PERM_T
            for b in range(rows // blk):
                stacked = jnp.concatenate(
                    [src[r, b * PERM_T:(b + 1) * PERM_T, :] for r in range(d)], axis=0)
                dst[b * blk:(b + 1) * blk, :] = jnp.dot(perm_ref[...], stacked, preferred_element_type=F32)
        for src, dst in ((l2_ref, ln2), (l3_ref, ln3)):
            d = src.shape[0]
            for r in range(d):
                dst[pl.ds(r, rows // d, stride=d), :] = src[r]
        l1, l2, l3 = l1_ref[...], ln2[...], ln3[...]
        m = jnp.maximum(jnp.maximum(l1, l2), l3)
        e1, e2, e3 = jnp.exp(l1 - m), jnp.exp(l2 - m), jnp.exp(l3 - m)
        inv = 1.0 / (e1 + e2 + e3)
        al1, al2, al3 = e1 * inv, e2 * inv, e3 * inv
        for h in range(n_heads):
            sl = slice(h * HEAD_DIM, (h + 1) * HEAD_DIM)
            o = (al1[:, h:h + 1] * o1_ref[:, sl].astype(F32)
                 + al2[:, h:h + 1] * on2[:, sl]
                 + al3[:, h:h + 1] * on3[:, sl])
            a_attn[:, sl] = (o * _silu(za_ref[:, sl].astype(F32))).astype(a_attn.dtype)
        emit_tile()

    pl.when(jnp.logical_not(first))(emit_tile)


def _branches(proj, outs, lses, a_four, w_attn, w_four, gate_bias, za_tile, gate_off,
              n_heads, tm=1024, tn=512):
    batch, seq, _ = proj.shape
    aw, fw = w_attn.shape[0], w_four.shape[0]
    d = w_attn.shape[1]
    assert gate_off % tn == 0
    gate_tile = gate_off // tn
    nj = d // tn
    bias = gate_bias.reshape(1, 2 * d).astype(F32)

    def nat(width, tile=0):
        return pl.BlockSpec((None, tm, width), lambda b, i, j: (b, i, tile))

    def cm(arr):
        dil = arr.shape[1]
        return pl.BlockSpec((None, dil, tm // dil, arr.shape[3]), lambda b, i, j: (b, 0, i, 0))

    perms = [_deinterleave_matrix(o.shape[1]) for o in outs[1:]]
    whole = lambda arr: pl.BlockSpec(arr.shape, lambda b, i, j: (0, 0))

    return pl.pallas_call(
        functools.partial(_branch_kernel, n_heads=n_heads),
        grid=(batch, seq // tm, nj),
        in_specs=[nat(aw), cm(outs[1]), cm(outs[2]), nat(LANES), cm(lses[1]), cm(lses[2]),
                  nat(aw, za_tile), nat(fw), whole(perms[0]), whole(perms[1]),
                  pl.BlockSpec((aw, tn), lambda b, i, j: (0, j)),
                  pl.BlockSpec((fw, tn), lambda b, i, j: (0, j)),
                  pl.BlockSpec((None, tm, tn), lambda b, i, j: (b, i, gate_tile + j)),
                  pl.BlockSpec((None, tm, tn), lambda b, i, j: (b, i, gate_tile + nj + j)),
                  pl.BlockSpec((1, tn), lambda b, i, j: (0, j)),
                  pl.BlockSpec((1, tn), lambda b, i, j: (0, nj + j))],
        out_specs=pl.BlockSpec((None, tm, tn), lambda b, i, j: (b, i, j)),
        out_shape=jax.ShapeDtypeStruct((batch, seq, d), BF16),
        scratch_shapes=[pltpu.VMEM((tm, aw), BF16),
                        pltpu.VMEM((tm, aw), F32), pltpu.VMEM((tm, aw), F32),
                        pltpu.VMEM((tm, LANES), F32), pltpu.VMEM((tm, LANES), F32)],
        compiler_params=_params(("parallel", "parallel", "arbitrary")),
        name="branches",
    )(outs[0], outs[1], outs[2], lses[0], lses[1], lses[2], proj, a_four, perms[0], perms[1],
      w_attn, w_four, proj, proj, bias, bias)


def _outproj_kernel(a_ref, w_ref, x_ref, *refs, nj, final_norm):
    o_ref = refs[-1]
    j = pl.program_id(1)
    tn = w_ref.shape[1]
    for jj in range(nj):
        @pl.when(j == jj)
        def _(jj=jj):
            y = x_ref[...] + jnp.dot(a_ref[...], w_ref[...], preferred_element_type=F32)
            if not (final_norm and jj == nj - 1):
                o_ref[:, jj * tn:(jj + 1) * tn] = y
                return
            g = refs[0][...]
            done = jj * tn
            n = done + tn
            chunk = 128
            for c in range(o_ref.shape[0] // chunk):
                rows = slice(c * chunk, (c + 1) * chunk)
                v = o_ref[rows, :done]
                yr = y[rows, :]
                ss = jnp.sum(v * v, axis=-1, keepdims=True) + jnp.sum(yr * yr, axis=-1, keepdims=True)
                inv = lax.rsqrt(ss / n + NORM_EPS)
                o_ref[rows, :done] = v * inv * g[:, :done]
                o_ref[rows, done:] = yr * inv * g[:, done:]


def _outproj(a, w, x, final_gain=None, tm=512, tn=1024):
    m, k = a.shape
    n = w.shape[1]
    nj = n // tn
    operands = [a, w, x]
    in_specs = [pl.BlockSpec((tm, k), lambda i, j: (i, 0)),
                pl.BlockSpec((k, tn), lambda i, j: (0, j)),
                pl.BlockSpec((tm, tn), lambda i, j: (i, j))]
    if final_gain is not None:
        operands.append(final_gain.reshape(1, n).astype(F32))
        in_specs.append(pl.BlockSpec((1, n), lambda i, j: (0, 0)))
    return pl.pallas_call(
        functools.partial(_outproj_kernel, nj=nj, final_norm=final_gain is not None),
        grid=(m // tm, nj),
        in_specs=in_specs,
        out_specs=pl.BlockSpec((tm, n), lambda i, j: (i, 0)),
        out_shape=jax.ShapeDtypeStruct((m, n), F32),
        compiler_params=_params(("parallel", "arbitrary")),
        name="outproj",
    )(*operands)


def _rotary_tables(seq):
    half = HEAD_DIM // 2
    inv_freq = ROPE_THETA ** (-jnp.arange(half, dtype=F32) * (2.0 / HEAD_DIM))
    ang = jnp.arange(seq, dtype=jnp.int32).astype(F32)[:, None] * inv_freq[None, :]
    cos, sin = jnp.cos(ang), jnp.sin(ang)
    return jnp.concatenate([cos, cos], axis=-1), jnp.concatenate([-sin, sin], axis=-1)


def kernel(x, norm_gain, w_in, gate_bias, w_branch_attn, w_branch_fourier, w_out, final_norm_gain):
    batch, seq, d = x.shape
    depth = norm_gain.shape[0]
    attn_width = w_branch_attn.shape[1]
    four_width = w_branch_fourier.shape[1]
    n_heads = attn_width // HEAD_DIM
    tn = attn_width
    assert four_width == 2 * tn and d % tn == 0
    assert w_in.shape[2] == (3 * N_GROUPS + 1) * tn + 2 * four_width + 2 * d
    assert all(w // (2 * dil) == ATTN_RADIUS for w, dil in DILATED_GROUPS)
    dilations = tuple(dil for _, dil in DILATED_GROUPS)
    assert dilations[0] == 1 and FFT_P in dilations
    q0, k0, v0 = 0, N_GROUPS, 2 * N_GROUPS
    za = 3 * N_GROUPS
    u0 = za + 1
    zf0 = u0 + 2
    g0 = zf0 + 2
    n_gate_tiles = 2 * d // tn

    cos_tab, sin_tab = _rotary_tables(seq)
    for layer in range(depth):
        *hs, w = _rmsnorm(x, norm_gain[layer], w_in[layer], dilations[1:])
        outs, lses = [], []
        a_four = None
        proj = None
        late_w = [w_branch_attn[layer], w_branch_fourier[layer]] + [None] * (len(dilations) - 2)
        late_bf = []
        for g, dil in enumerate(dilations):
            tiles = [q0 + g, k0 + g, v0 + g]
            if dil == 1:
                tiles += [za, zf0, zf0 + 1] + [g0 + t for t in range(n_gate_tiles)]
            q_tile = 0
            if dil == FFT_P:
                tiles = [u0, u0 + 1] + tiles
                q_tile = 2
            pg = _inproj(hs[g].reshape(batch * seq, d), w, tiles, q_tile, cos_tab, sin_tab, seq, dil, tn=tn)
            cols = pg.shape[1]
            o_g, lse_g, *w_bf = _banded_attention(pg.reshape(batch * dil, seq // dil, cols), q_tile, n_heads,
                                                  late_w[g])
            late_bf += w_bf
            if dil == 1:
                proj = pg.reshape(batch, seq, cols)
                outs.append(o_g)
                lses.append(lse_g)
            else:
                outs.append(o_g.reshape(batch, dil, seq // dil, attn_width))
                lses.append(lse_g.reshape(batch, dil, seq // dil, LANES))
            if dil == FFT_P:
                a_four, w_out_bf = _fourier_mix(pg.reshape(batch, dil, seq // dil, cols), 0, proj, 2,
                                                batch, seq, four_width, w_out[layer])
        mixed = _branches(proj, outs, lses, a_four, late_bf[0], late_bf[1], gate_bias[layer],
                          3, 6 * tn, n_heads)
        x = _outproj(mixed.reshape(batch * seq, d), w_out_bf, x.reshape(batch * seq, d),
                     final_norm_gain if layer == depth - 1 else None).reshape(batch, seq, d)
    return x
```

```python
import functools

import numpy as np
import jax
import jax.numpy as jnp
from jax import lax
from jax.experimental import pallas as pl
from jax.experimental.pallas import tpu as pltpu

HEAD_DIM = 128
DILATED_GROUPS = ((128, 1), (512, 4), (2048, 16))
N_GROUPS = len(DILATED_GROUPS)
FOURIER_GROUPS = 8
ROPE_THETA = 10000.0
NORM_EPS = 1e-6
NEG_INF = -1e30
ATTN_RADIUS = 64
FFT_P = 16
FFT_F2_BLOCK = 16
LANES = 128
VMEM_LIMIT = 56 * 1024 * 1024

F32 = jnp.float32
BF16 = jnp.bfloat16


def _params(sem, vmem=VMEM_LIMIT):
    return pltpu.CompilerParams(dimension_semantics=sem, vmem_limit_bytes=vmem)


def _rmsnorm_kernel(x_ref, g_ref, w_ref, o_ref, *refs, dilations):
    od_refs, wb_ref, ybuf = refs[:-2], refs[-2], refs[-1]
    x = x_ref[...].astype(F32)
    ms = jnp.mean(x * x, axis=-1, keepdims=True)
    y = x * lax.rsqrt(ms + NORM_EPS) * g_ref[...]
    o_ref[...] = y.astype(o_ref.dtype)
    rows = y.shape[0]
    chunks = [slice(c * LANES, (c + 1) * LANES) for c in range(y.shape[1] // LANES)]
    for c, sl in enumerate(chunks):
        ybuf[c] = y[:, sl]
    for d, od_ref in zip(dilations, od_refs):
        for r in range(d):
            for c, sl in enumerate(chunks):
                od_ref[r, :, sl] = ybuf[c, pl.ds(r, rows // d, stride=d), :].astype(od_ref.dtype)
    wb_ref[...] = w_ref[...].astype(wb_ref.dtype)


def _rmsnorm(x, gain, w, dilations, tm=256):
    batch, seq, d = x.shape
    steps = batch * (seq // tm)
    wk, wn = w.shape
    assert wk % steps == 0
    wr = wk // steps
    out_specs = [pl.BlockSpec((None, tm, d), lambda b, i: (b, i, 0))]
    out_shape = [jax.ShapeDtypeStruct((batch, seq, d), BF16)]
    for dil in dilations:
        out_specs.append(pl.BlockSpec((None, dil, tm // dil, d), lambda b, i: (b, 0, i, 0)))
        out_shape.append(jax.ShapeDtypeStruct((batch, dil, seq // dil, d), BF16))
    w_index = lambda b, i: (b * (seq // tm) + i, 0)
    out_specs.append(pl.BlockSpec((wr, wn), w_index))
    out_shape.append(jax.ShapeDtypeStruct((wk, wn), BF16))
    return pl.pallas_call(
        functools.partial(_rmsnorm_kernel, dilations=tuple(dilations)),
        grid=(batch, seq // tm),
        in_specs=[pl.BlockSpec((None, tm, d), lambda b, i: (b, i, 0)),
                  pl.BlockSpec((1, d), lambda b, i: (0, 0)),
                  pl.BlockSpec((wr, wn), w_index)],
        out_specs=out_specs,
        out_shape=out_shape,
        scratch_shapes=[pltpu.VMEM((d // LANES, tm, LANES), F32)],
        compiler_params=_params(("parallel", "parallel")),
        name="rmsnorm",
    )(x, gain.reshape(1, d).astype(F32), w)


def _inproj_kernel(tiles_ref, h_ref, w_ref, cos0_ref, cos1_ref, sin0_ref, sin1_ref, o_ref, *, rot_lo):
    del tiles_ref
    j = pl.program_id(1)
    is_rot = (j >= rot_lo) & (j < rot_lo + 2)
    tm, tn = o_ref.shape
    half = tm // 2
    chunk = 2 * HEAD_DIM

    @pl.when(is_rot)
    def _():
        scale = jnp.where(j == rot_lo, HEAD_DIM ** -0.5, 1.0).astype(F32)
        tabs = [(cos0_ref[...] * scale, sin0_ref[...] * scale), (cos1_ref[...] * scale, sin1_ref[...] * scale)]
        for ck in range(tn // chunk):
            acc = jnp.dot(h_ref[...], w_ref[:, ck * chunk:(ck + 1) * chunk], preferred_element_type=F32)
            for a, (c, s) in enumerate(tabs):
                rows = slice(a * half, (a + 1) * half)
                for hd in range(chunk // HEAD_DIM):
                    xh = acc[rows, hd * HEAD_DIM:(hd + 1) * HEAD_DIM]
                    lo = ck * chunk + hd * HEAD_DIM
                    o_ref[rows, lo:lo + HEAD_DIM] = (
                        xh * c + pltpu.roll(xh, HEAD_DIM // 2, axis=1) * s).astype(o_ref.dtype)

    @pl.when(jnp.logical_not(is_rot))
    def _():
        o_ref[...] = jnp.dot(h_ref[...], w_ref[...], preferred_element_type=F32).astype(o_ref.dtype)


def _inproj(h, w, col_tiles, rot_lo, cos_tab, sin_tab, seq, dilation, tm=1024, tn=1024):
    m, k = h.shape
    sub = seq // dilation
    half = tm // 2
    assert sub % half == 0
    cos_v = cos_tab.reshape(sub, dilation * HEAD_DIM)
    sin_v = sin_tab.reshape(sub, dilation * HEAD_DIM)

    def tab_spec(a):
        def index(i, j, t):
            pos = (i * tm + a * half) % seq
            return ((pos % sub) // half, pos // sub)
        return pl.BlockSpec((half, HEAD_DIM), index)

    grid_spec = pltpu.PrefetchScalarGridSpec(
        num_scalar_prefetch=1,
        grid=(m // tm, len(col_tiles)),
        in_specs=[pl.BlockSpec((tm, k), lambda i, j, t: (i, 0)),
                  pl.BlockSpec((k, tn), lambda i, j, t: (0, t[j])),
                  tab_spec(0), tab_spec(1), tab_spec(0), tab_spec(1)],
        out_specs=pl.BlockSpec((tm, tn), lambda i, j, t: (i, j)),
    )
    return pl.pallas_call(
        functools.partial(_inproj_kernel, rot_lo=rot_lo),
        grid_spec=grid_spec,
        out_shape=jax.ShapeDtypeStruct((m, len(col_tiles) * tn), BF16),
        compiler_params=_params(("parallel", "arbitrary")),
        name="inproj",
    )(jnp.asarray(np.asarray(col_tiles, np.int32)), h, w, cos_v, cos_v, sin_v, sin_v)


def _attn_kernel(q_ref, kp_ref, kc_ref, kn_ref, vp_ref, vc_ref, vn_ref, *refs, tq, sub, n_heads, rounds_w):
    if rounds_w:
        w_ref, o_ref, lse_ref, wb_ref, kbuf, vbuf = refs
        wb_ref[...] = w_ref[...].astype(wb_ref.dtype)
    else:
        o_ref, lse_ref, kbuf, vbuf = refs
    it = pl.program_id(1)
    r = ATTN_RADIUS
    qb = 2 * r
    kb = 4 * r
    kbuf[0:r, :] = kp_ref[...]
    kbuf[r:r + tq, :] = kc_ref[...]
    kbuf[r + tq:, :] = kn_ref[...]
    vbuf[0:r, :] = vp_ref[...]
    vbuf[r:r + tq, :] = vc_ref[...]
    vbuf[r + tq:, :] = vn_ref[...]

    def body(i, carry):
        r0 = pl.multiple_of(i * qb, qb)
        row = lax.broadcasted_iota(jnp.int32, (qb, kb), 0)
        col = lax.broadcasted_iota(jnp.int32, (qb, kb), 1)
        delta = col - row
        kpos = col + (it * tq + i * qb - r)
        valid = (delta >= 0) & (delta <= 2 * r) & (kpos >= 0) & (kpos < sub)
        lane = lax.broadcasted_iota(jnp.int32, (qb, LANES), 1)
        lse_tile = jnp.zeros((qb, LANES), F32)
        for h in range(n_heads):
            sl = slice(h * HEAD_DIM, (h + 1) * HEAD_DIM)
            q = q_ref[pl.ds(r0, qb), sl]
            k = kbuf[pl.ds(r0, kb), sl]
            v = vbuf[pl.ds(r0, kb), sl]
            s = lax.dot_general(q, k, (((1,), (1,)), ((), ())), preferred_element_type=F32)
            s = jnp.where(valid, s, NEG_INF)
            m = jnp.max(s, axis=-1, keepdims=True)
            p = jnp.exp(s - m)
            l = jnp.sum(p, axis=-1, keepdims=True)
            o = jnp.dot(p.astype(BF16), v, preferred_element_type=F32)
            o_ref[pl.ds(r0, qb), sl] = (o / l).astype(o_ref.dtype)
            lse_tile = jnp.where(lane == h, m + jnp.log(l), lse_tile)
        lse_ref[pl.ds(r0, qb), :] = lse_tile
        return carry

    lax.fori_loop(0, tq // qb, body, 0, unroll=True)


def _round_spec(w, steps, index):
    wk, wn = w.shape
    assert wk % steps == 0
    return pl.BlockSpec((wk // steps, wn), index)


def _banded_attention(qkv, q_tile, n_heads, w=None, tq=1024):
    n, sub, _ = qkv.shape
    width = n_heads * HEAD_DIM
    tq = min(tq, sub)
    r = ATTN_RADIUS
    per = tq // r
    nt = sub // tq
    extra_in, extra_spec, extra_shape = [], [], []
    if w is not None:
        extra_in = [w]
        extra_spec = [_round_spec(w, n * nt, lambda b, t: (b * nt + t, 0))]
        extra_shape = [jax.ShapeDtypeStruct(w.shape, BF16)]

    def cur(off):
        return pl.BlockSpec((None, tq, width), lambda b, t: (b, t, q_tile + off))

    def prev(off):
        return pl.BlockSpec((None, r, width), lambda b, t: (b, jnp.maximum(t * per - 1, 0), q_tile + off))

    def nxt(off):
        return pl.BlockSpec((None, r, width),
                            lambda b, t: (b, jnp.minimum((t + 1) * per, sub // r - 1), q_tile + off))

    return pl.pallas_call(
        functools.partial(_attn_kernel, tq=tq, sub=sub, n_heads=n_heads, rounds_w=w is not None),
        grid=(n, nt),
        in_specs=[cur(0), prev(1), cur(1), nxt(1), prev(2), cur(2), nxt(2)] + extra_spec,
        out_specs=[pl.BlockSpec((None, tq, width), lambda b, t: (b, t, 0)),
                   pl.BlockSpec((None, tq, LANES), lambda b, t: (b, t, 0))] + extra_spec,
        out_shape=[jax.ShapeDtypeStruct((n, sub, width), BF16),
                   jax.ShapeDtypeStruct((n, sub, LANES), F32)] + extra_shape,
        scratch_shapes=[pltpu.VMEM((tq + 2 * r, width), BF16),
                        pltpu.VMEM((tq + 2 * r, width), BF16)],
        compiler_params=_params(("parallel", "parallel")),
        name=f"attn_sub{sub}",
    )(qkv, qkv, qkv, qkv, qkv, qkv, qkv, *extra_in)


def _dft_tables(seq, gdim):
    p, fb = FFT_P, FFT_F2_BLOCK
    q = seq // p
    c = np.arange(gdim, dtype=np.float64)
    ang = 2 * np.pi * np.outer(c, c) / gdim
    wc = np.concatenate([np.cos(ang), np.sin(ang)], axis=1) / np.sqrt(gdim)
    aq = 2 * np.pi * (np.outer(np.arange(q), np.arange(q)) % q) / q
    cq, sq = np.cos(aq) / np.sqrt(q), np.sin(aq) / np.sqrt(q)
    w1 = np.stack([cq, -(sq + cq), sq - cq])
    f = (q * np.arange(p)[None, :, None, None]
         + fb * np.arange(q // fb)[:, None, None, None] + np.arange(fb)[None, None, :, None])
    phi = 2 * np.pi * ((f * np.arange(p)[None, None, None, :]) % seq) / seq
    eye = np.eye(fb)
    parts = [np.einsum('afjs,jk->afjsk', t / np.sqrt(p), eye) for t in (np.cos(phi), np.sin(phi))]
    big = np.stack(parts, axis=3).reshape(q // fb, p * fb, 2 * p * fb)
    as_bf16 = lambda a: jnp.asarray(a.astype(np.float32)).astype(BF16)
    return as_bf16(wc), as_bf16(w1), as_bf16(big)


def _fft1_kernel(u_ref, wc_ref, w1_ref, w_ref, h_ref, wb_ref, *, gdim):
    wb_ref[...] = w_ref[...].astype(wb_ref.dtype)
    for g in range(u_ref.shape[1] // gdim):
        sl = slice(g * gdim, (g + 1) * gdim)
        ab = jnp.dot(u_ref[:, sl], wc_ref[...], preferred_element_type=F32)
        a, b = ab[:, :gdim], ab[:, gdim:]
        k1 = jnp.dot(w1_ref[0], (a - b).astype(BF16), preferred_element_type=F32)
        k2 = jnp.dot(w1_ref[1], a.astype(BF16), preferred_element_type=F32)
        k3 = jnp.dot(w1_ref[2], b.astype(BF16), preferred_element_type=F32)
        h_ref[0, :, sl] = (k1 - k3).astype(h_ref.dtype)
        h_ref[1, :, sl] = (k1 + k2).astype(h_ref.dtype)


def _sigmoid(z):
    return 0.5 * jnp.tanh(0.5 * z) + 0.5


def _silu(z):
    h = 0.5 * z
    return h + h * jnp.tanh(h)


def _fft2_kernel(h_ref, big_ref, z_ref, o_ref):
    parts, p, fb, width = h_ref.shape
    rhs = h_ref[...].reshape(parts * p * fb, width)
    y = jnp.dot(big_ref[...], rhs, preferred_element_type=F32)
    o_ref[...] = (y.reshape(p, fb, width) * _silu(z_ref[...].astype(F32))).astype(o_ref.dtype)


def _fourier_mix(u_cm, u_tile, proj, z_tile, batch, seq, width, w):
    gdim = width // FOURIER_GROUPS
    p, fb = FFT_P, FFT_F2_BLOCK
    q = seq // p
    wc, w1, big = _dft_tables(seq, gdim)
    w_spec = _round_spec(w, batch * p, lambda b, s: (b * p + s, 0))
    h, w_bf = pl.pallas_call(
        functools.partial(_fft1_kernel, gdim=gdim),
        grid=(batch, p),
        in_specs=[pl.BlockSpec((None, None, q, width), lambda b, s: (b, s, 0, u_tile)),
                  pl.BlockSpec(wc.shape, lambda b, s: (0, 0)),
                  pl.BlockSpec(w1.shape, lambda b, s: (0, 0, 0)),
                  w_spec],
        out_specs=[pl.BlockSpec((None, 2, None, q, width), lambda b, s: (b, 0, s, 0, 0)), w_spec],
        out_shape=[jax.ShapeDtypeStruct((batch, 2, p, q, width), BF16),
                   jax.ShapeDtypeStruct(w.shape, BF16)],
        compiler_params=_params(("parallel", "parallel")),
        name="fft_seq1",
    )(u_cm, wc, w1, w)
    y = pl.pallas_call(
        _fft2_kernel,
        grid=(batch, q // fb),
        in_specs=[pl.BlockSpec((None, 2, p, fb, width), lambda b, a: (b, 0, 0, a, 0)),
                  pl.BlockSpec((None, p * fb, 2 * p * fb), lambda b, a: (a, 0, 0)),
                  pl.BlockSpec((None, p, None, fb, width), lambda b, a: (b, 0, a, 0, z_tile))],
        out_specs=pl.BlockSpec((None, p, None, fb, width), lambda b, a: (b, 0, a, 0, 0)),
        out_shape=jax.ShapeDtypeStruct((batch, p, q // fb, fb, width), BF16),
        compiler_params=_params(("parallel", "parallel")),
        name="fft_seq2",
    )(h, big, proj.reshape(batch, p, q // fb, fb, proj.shape[-1]))
    return y.reshape(batch, seq, width), w_bf


PERM_T = 16


def _deinterleave_matrix(d):
    p = np.zeros((d * PERM_T, d * PERM_T), np.float32)
    for r in range(d):
        for t in range(PERM_T):
            p[t * d + r, r * PERM_T + t] = 1.0
    return jnp.asarray(p).astype(BF16)


def _branch_kernel(o1_ref, o2_ref, o3_ref, l1_ref, l2_ref, l3_ref, za_ref, af_ref, p2_ref, p3_ref,
                   wa_ref, wf_ref, ga_ref, gf_ref, ba_ref, bf_ref, out_ref,
                   a_attn, on2, on3, ln2, ln3, *, n_heads):
    def emit_tile():
        br_f = jnp.dot(af_ref[...], wf_ref[...], preferred_element_type=F32)
        br_a = jnp.dot(a_attn[...], wa_ref[...], preferred_element_type=F32)
        g_a = _sigmoid(ga_ref[...].astype(F32) + ba_ref[...])
        g_f = _sigmoid(gf_ref[...].astype(F32) + bf_ref[...])
        out_ref[...] = (g_a * br_a + g_f * br_f).astype(out_ref.dtype)

    first = pl.program_id(2) == 0

    @pl.when(first)
    def _():
        rows = ln2.shape[0]
        for src, perm_ref, dst in ((o2_ref, p2_ref, on2), (o3_ref, p3_ref, on3)):
            d = src.shape[0]
            blk = d * PERM_T
            for b in range(rows // blk):
                stacked = jnp.concatenate(
                    [src[r, b * PERM_T:(b + 1) * PERM_T, :] for r in range(d)], axis=0)
                dst[b * blk:(b + 1) * blk, :] = jnp.dot(perm_ref[...], stacked, preferred_element_type=F32)
        for src, dst in ((l2_ref, ln2), (l3_ref, ln3)):
            d = src.shape[0]
            for r in range(d):
                dst[pl.ds(r, rows // d, stride=d), :] = src[r]
        l1, l2, l3 = l1_ref[...], ln2[...], ln3[...]
        m = jnp.maximum(jnp.maximum(l1, l2), l3)
        e1, e2, e3 = jnp.exp(l1 - m), jnp.exp(l2 - m), jnp.exp(l3 - m)
        inv = 1.0 / (e1 + e2 + e3)
        al1, al2, al3 = e1 * inv, e2 * inv, e3 * inv
        for h in range(n_heads):
            sl = slice(h * HEAD_DIM, (h + 1) * HEAD_DIM)
            o = (al1[:, h:h + 1] * o1_ref[:, sl].astype(F32)
                 + al2[:, h:h + 1] * on2[:, sl]
                 + al3[:, h:h + 1] * on3[:, sl])
            a_attn[:, sl] = (o * _silu(za_ref[:, sl].astype(F32))).astype(a_attn.dtype)
        emit_tile()

    pl.when(jnp.logical_not(first))(emit_tile)


def _branches(proj, outs, lses, a_four, w_attn, w_four, gate_bias, za_tile, gate_off,
              n_heads, tm=512, tn=1024):
    batch, seq, _ = proj.shape
    aw, fw = w_attn.shape[0], w_four.shape[0]
    d = w_attn.shape[1]
    assert gate_off % tn == 0
    gate_tile = gate_off // tn
    nj = d // tn
    bias = gate_bias.reshape(1, 2 * d).astype(F32)

    def nat(width, tile=0):
        return pl.BlockSpec((None, tm, width), lambda b, i, j: (b, i, tile))

    def cm(arr):
        dil = arr.shape[1]
        return pl.BlockSpec((None, dil, tm // dil, arr.shape[3]), lambda b, i, j: (b, 0, i, 0))

    perms = [_deinterleave_matrix(o.shape[1]) for o in outs[1:]]
    whole = lambda arr: pl.BlockSpec(arr.shape, lambda b, i, j: (0, 0))

    return pl.pallas_call(
        functools.partial(_branch_kernel, n_heads=n_heads),
        grid=(batch, seq // tm, nj),
        in_specs=[nat(aw), cm(outs[1]), cm(outs[2]), nat(LANES), cm(lses[1]), cm(lses[2]),
                  nat(aw, za_tile), nat(fw), whole(perms[0]), whole(perms[1]),
                  pl.BlockSpec((aw, tn), lambda b, i, j: (0, j)),
                  pl.BlockSpec((fw, tn), lambda b, i, j: (0, j)),
                  pl.BlockSpec((None, tm, tn), lambda b, i, j: (b, i, gate_tile + j)),
                  pl.BlockSpec((None, tm, tn), lambda b, i, j: (b, i, gate_tile + nj + j)),
                  pl.BlockSpec((1, tn), lambda b, i, j: (0, j)),
                  pl.BlockSpec((1, tn), lambda b, i, j: (0, nj + j))],
        out_specs=pl.BlockSpec((None, tm, tn), lambda b, i, j: (b, i, j)),
        out_shape=jax.ShapeDtypeStruct((batch, seq, d), BF16),
        scratch_shapes=[pltpu.VMEM((tm, aw), BF16),
                        pltpu.VMEM((tm, aw), F32), pltpu.VMEM((tm, aw), F32),
                        pltpu.VMEM((tm, LANES), F32), pltpu.VMEM((tm, LANES), F32)],
        compiler_params=_params(("parallel", "parallel", "arbitrary")),
        name="branches",
    )(outs[0], outs[1], outs[2], lses[0], lses[1], lses[2], proj, a_four, perms[0], perms[1],
      w_attn, w_four, proj, proj, bias, bias)


def _outproj_kernel(a_ref, w_ref, x_ref, *refs, nj, final_norm):
    o_ref = refs[-1]
    j = pl.program_id(1)
    tn = w_ref.shape[1]
    for jj in range(nj):
        @pl.when(j == jj)
        def _(jj=jj):
            y = x_ref[...] + jnp.dot(a_ref[...], w_ref[...], preferred_element_type=F32)
            if not (final_norm and jj == nj - 1):
                o_ref[:, jj * tn:(jj + 1) * tn] = y
                return
            g = refs[0][...]
            done = jj * tn
            n = done + tn
            chunk = 128
            for c in range(o_ref.shape[0] // chunk):
                rows = slice(c * chunk, (c + 1) * chunk)
                v = o_ref[rows, :done]
                yr = y[rows, :]
                ss = jnp.sum(v * v, axis=-1, keepdims=True) + jnp.sum(yr * yr, axis=-1, keepdims=True)
                inv = lax.rsqrt(ss / n + NORM_EPS)
                o_ref[rows, :done] = v * inv * g[:, :done]
                o_ref[rows, done:] = yr * inv * g[:, done:]


def _outproj(a, w, x, final_gain=None, tm=512, tn=1024):
    m, k = a.shape
    n = w.shape[1]
    nj = n // tn
    operands = [a, w, x]
    in_specs = [pl.BlockSpec((tm, k), lambda i, j: (i, 0)),
                pl.BlockSpec((k, tn), lambda i, j: (0, j)),
                pl.BlockSpec((tm, tn), lambda i, j: (i, j))]
    if final_gain is not None:
        operands.append(final_gain.reshape(1, n).astype(F32))
        in_specs.append(pl.BlockSpec((1, n), lambda i, j: (0, 0)))
    return pl.pallas_call(
        functools.partial(_outproj_kernel, nj=nj, final_norm=final_gain is not None),
        grid=(m // tm, nj),
        in_specs=in_specs,
        out_specs=pl.BlockSpec((tm, n), lambda i, j: (i, 0)),
        out_shape=jax.ShapeDtypeStruct((m, n), F32),
        compiler_params=_params(("parallel", "arbitrary")),
        name="outproj",
    )(*operands)


def _rotary_tables(seq):
    half = HEAD_DIM // 2
    inv_freq = ROPE_THETA ** (-jnp.arange(half, dtype=F32) * (2.0 / HEAD_DIM))
    ang = jnp.arange(seq, dtype=jnp.int32).astype(F32)[:, None] * inv_freq[None, :]
    cos, sin = jnp.cos(ang), jnp.sin(ang)
    return jnp.concatenate([cos, cos], axis=-1), jnp.concatenate([-sin, sin], axis=-1)


def kernel(x, norm_gain, w_in, gate_bias, w_branch_attn, w_branch_fourier, w_out, final_norm_gain):
    batch, seq, d = x.shape
    depth = norm_gain.shape[0]
    attn_width = w_branch_attn.shape[1]
    four_width = w_branch_fourier.shape[1]
    n_heads = attn_width // HEAD_DIM
    tn = attn_width
    assert four_width == 2 * tn and d % tn == 0
    assert w_in.shape[2] == (3 * N_GROUPS + 1) * tn + 2 * four_width + 2 * d
    assert all(w // (2 * dil) == ATTN_RADIUS for w, dil in DILATED_GROUPS)
    dilations = tuple(dil for _, dil in DILATED_GROUPS)
    assert dilations[0] == 1 and FFT_P in dilations
    q0, k0, v0 = 0, N_GROUPS, 2 * N_GROUPS
    za = 3 * N_GROUPS
    u0 = za + 1
    zf0 = u0 + 2
    g0 = zf0 + 2
    n_gate_tiles = 2 * d // tn

    cos_tab, sin_tab = _rotary_tables(seq)
    for layer in range(depth):
        *hs, w = _rmsnorm(x, norm_gain[layer], w_in[layer], dilations[1:])
        outs, lses = [], []
        a_four = None
        proj = None
        late_w = [w_branch_attn[layer], w_branch_fourier[layer]] + [None] * (len(dilations) - 2)
        late_bf = []
        for g, dil in enumerate(dilations):
            tiles = [q0 + g, k0 + g, v0 + g]
            if dil == 1:
                tiles += [za, zf0, zf0 + 1] + [g0 + t for t in range(n_gate_tiles)]
            q_tile = 0
            if dil == FFT_P:
                tiles = [u0, u0 + 1] + tiles
                q_tile = 2
            pg = _inproj(hs[g].reshape(batch * seq, d), w, tiles, q_tile, cos_tab, sin_tab, seq, dil, tn=tn)
            cols = pg.shape[1]
            o_g, lse_g, *w_bf = _banded_attention(pg.reshape(batch * dil, seq // dil, cols), q_tile, n_heads,
                                                  late_w[g])
            late_bf += w_bf
            if dil == 1:
                proj = pg.reshape(batch, seq, cols)
                outs.append(o_g)
                lses.append(lse_g)
            else:
                outs.append(o_g.reshape(batch, dil, seq // dil, attn_width))
                lses.append(lse_g.reshape(batch, dil, seq // dil, LANES))
            if dil == FFT_P:
                a_four, w_out_bf = _fourier_mix(pg.reshape(batch, dil, seq // dil, cols), 0, proj, 2,
                                                batch, seq, four_width, w_out[layer])
        mixed = _branches(proj, outs, lses, a_four, late_bf[0], late_bf[1], gate_bias[layer],
                          3, 6 * tn, n_heads)
        x = _outproj(mixed.reshape(batch * seq, d), w_out_bf, x.reshape(batch * seq, d),
                     final_norm_gain if layer == depth - 1 else None).reshape(batch, seq, d)
    return x
```

```python
import functools

import numpy as np
import jax
import jax.numpy as jnp
from jax import lax
from jax.experimental import pallas as pl
from jax.experimental.pallas import tpu as pltpu

HEAD_DIM = 128
DILATED_GROUPS = ((128, 1), (512, 4), (2048, 16))
N_GROUPS = len(DILATED_GROUPS)
FOURIER_GROUPS = 8
ROPE_THETA = 10000.0
NORM_EPS = 1e-6
NEG_INF = -1e30
ATTN_RADIUS = 64
FFT_P = 16
FFT_F2_BLOCK = 16
LANES = 128
VMEM_LIMIT = 56 * 1024 * 1024

F32 = jnp.float32
BF16 = jnp.bfloat16


def _params(sem, vmem=VMEM_LIMIT):
    return pltpu.CompilerParams(dimension_semantics=sem, vmem_limit_bytes=vmem)


def _rmsnorm_kernel(x_ref, g_ref, w_ref, o_ref, *refs, dilations):
    od_refs, wb_ref, ybuf = refs[:-2], refs[-2], refs[-1]
    x = x_ref[...].astype(F32)
    ms = jnp.mean(x * x, axis=-1, keepdims=True)
    y = x * lax.rsqrt(ms + NORM_EPS) * g_ref[...]
    o_ref[...] = y.astype(o_ref.dtype)
    rows = y.shape[0]
    chunks = [slice(c * LANES, (c + 1) * LANES) for c in range(y.shape[1] // LANES)]
    for c, sl in enumerate(chunks):
        ybuf[c] = y[:, sl]
    for d, od_ref in zip(dilations, od_refs):
        for r in range(d):
            for c, sl in enumerate(chunks):
                od_ref[r, :, sl] = ybuf[c, pl.ds(r, rows // d, stride=d), :].astype(od_ref.dtype)
    wb_ref[...] = w_ref[...].astype(wb_ref.dtype)


def _rmsnorm(x, gain, w, dilations, tm=256):
    batch, seq, d = x.shape
    steps = batch * (seq // tm)
    wk, wn = w.shape
    assert wk % steps == 0
    wr = wk // steps
    out_specs = [pl.BlockSpec((None, tm, d), lambda b, i: (b, i, 0))]
    out_shape = [jax.ShapeDtypeStruct((batch, seq, d), BF16)]
    for dil in dilations:
        out_specs.append(pl.BlockSpec((None, dil, tm // dil, d), lambda b, i: (b, 0, i, 0)))
        out_shape.append(jax.ShapeDtypeStruct((batch, dil, seq // dil, d), BF16))
    w_index = lambda b, i: (b * (seq // tm) + i, 0)
    out_specs.append(pl.BlockSpec((wr, wn), w_index))
    out_shape.append(jax.ShapeDtypeStruct((wk, wn), BF16))
    return pl.pallas_call(
        functools.partial(_rmsnorm_kernel, dilations=tuple(dilations)),
        grid=(batch, seq // tm),
        in_specs=[pl.BlockSpec((None, tm, d), lambda b, i: (b, i, 0)),
                  pl.BlockSpec((1, d), lambda b, i: (0, 0)),
                  pl.BlockSpec((wr, wn), w_index)],
        out_specs=out_specs,
        out_shape=out_shape,
        scratch_shapes=[pltpu.VMEM((d // LANES, tm, LANES), F32)],
        compiler_params=_params(("parallel", "parallel")),
        name="rmsnorm",
    )(x, gain.reshape(1, d).astype(F32), w)


def _inproj_kernel(tiles_ref, h_ref, w_ref, cos0_ref, cos1_ref, sin0_ref, sin1_ref, o_ref, *, rot_lo):
    del tiles_ref
    j = pl.program_id(1)
    is_rot = (j >= rot_lo) & (j < rot_lo + 2)
    tm, tn = o_ref.shape
    half = tm // 2
    chunk = 2 * HEAD_DIM

    @pl.when(is_rot)
    def _():
        scale = jnp.where(j == rot_lo, HEAD_DIM ** -0.5, 1.0).astype(F32)
        tabs = [(cos0_ref[...] * scale, sin0_ref[...] * scale), (cos1_ref[...] * scale, sin1_ref[...] * scale)]
        for ck in range(tn // chunk):
            acc = jnp.dot(h_ref[...], w_ref[:, ck * chunk:(ck + 1) * chunk], preferred_element_type=F32)
            for a, (c, s) in enumerate(tabs):
                rows = slice(a * half, (a + 1) * half)
                for hd in range(chunk // HEAD_DIM):
                    xh = acc[rows, hd * HEAD_DIM:(hd + 1) * HEAD_DIM]
                    lo = ck * chunk + hd * HEAD_DIM
                    o_ref[rows, lo:lo + HEAD_DIM] = (
                        xh * c + pltpu.roll(xh, HEAD_DIM // 2, axis=1) * s).astype(o_ref.dtype)

    @pl.when(jnp.logical_not(is_rot))
    def _():
        o_ref[...] = jnp.dot(h_ref[...], w_ref[...], preferred_element_type=F32).astype(o_ref.dtype)


def _inproj(h, w, col_tiles, rot_lo, cos_tab, sin_tab, seq, dilation, tm=1024, tn=1024):
    m, k = h.shape
    sub = seq // dilation
    half = tm // 2
    assert sub % half == 0
    cos_v = cos_tab.reshape(sub, dilation * HEAD_DIM)
    sin_v = sin_tab.reshape(sub, dilation * HEAD_DIM)

    def tab_spec(a):
        def index(i, j, t):
            pos = (i * tm + a * half) % seq
            return ((pos % sub) // half, pos // sub)
        return pl.BlockSpec((half, HEAD_DIM), index)

    grid_spec = pltpu.PrefetchScalarGridSpec(
        num_scalar_prefetch=1,
        grid=(m // tm, len(col_tiles)),
        in_specs=[pl.BlockSpec((tm, k), lambda i, j, t: (i, 0)),
                  pl.BlockSpec((k, tn), lambda i, j, t: (0, t[j])),
                  tab_spec(0), tab_spec(1), tab_spec(0), tab_spec(1)],
        out_specs=pl.BlockSpec((tm, tn), lambda i, j, t: (i, j)),
    )
    return pl.pallas_call(
        functools.partial(_inproj_kernel, rot_lo=rot_lo),
        grid_spec=grid_spec,
        out_shape=jax.ShapeDtypeStruct((m, len(col_tiles) * tn), BF16),
        compiler_params=_params(("parallel", "arbitrary")),
        name="inproj",
    )(jnp.asarray(np.asarray(col_tiles, np.int32)), h, w, cos_v, cos_v, sin_v, sin_v)


def _attn_kernel(q_ref, kp_ref, kc_ref, kn_ref, vp_ref, vc_ref, vn_ref, w_ref, o_ref, lse_ref, wb_ref,
                 kbuf, vbuf, *, tq, sub, n_heads):
    wb_ref[...] = w_ref[...].astype(wb_ref.dtype)
    it = pl.program_id(1)
    r = ATTN_RADIUS
    qb = 2 * r
    kb = 4 * r
    kbuf[0:r, :] = kp_ref[...]
    kbuf[r:r + tq, :] = kc_ref[...]
    kbuf[r + tq:, :] = kn_ref[...]
    vbuf[0:r, :] = vp_ref[...]
    vbuf[r:r + tq, :] = vc_ref[...]
    vbuf[r + tq:, :] = vn_ref[...]

    def body(i, carry):
        r0 = pl.multiple_of(i * qb, qb)
        row = lax.broadcasted_iota(jnp.int32, (qb, kb), 0)
        col = lax.broadcasted_iota(jnp.int32, (qb, kb), 1)
        delta = col - row
        kpos = col + (it * tq + i * qb - r)
        valid = (delta >= 0) & (delta <= 2 * r) & (kpos >= 0) & (kpos < sub)
        lane = lax.broadcasted_iota(jnp.int32, (qb, LANES), 1)
        lse_tile = jnp.zeros((qb, LANES), F32)
        for h in range(n_heads):
            sl = slice(h * HEAD_DIM, (h + 1) * HEAD_DIM)
            q = q_ref[pl.ds(r0, qb), sl]
            k = kbuf[pl.ds(r0, kb), sl]
            v = vbuf[pl.ds(r0, kb), sl]
            s = lax.dot_general(q, k, (((1,), (1,)), ((), ())), preferred_element_type=F32)
            s = jnp.where(valid, s, NEG_INF)
            m = jnp.max(s, axis=-1, keepdims=True)
            p = jnp.exp(s - m)
            l = jnp.sum(p, axis=-1, keepdims=True)
            o = jnp.dot(p.astype(BF16), v, preferred_element_type=F32)
            o_ref[pl.ds(r0, qb), sl] = (o / l).astype(o_ref.dtype)
            lse_tile = jnp.where(lane == h, m + jnp.log(l), lse_tile)
        lse_ref[pl.ds(r0, qb), :] = lse_tile
        return carry

    lax.fori_loop(0, tq // qb, body, 0, unroll=True)


def _banded_attention(qkv, q_tile, n_heads, w, tq=1024):
    n, sub, _ = qkv.shape
    width = n_heads * HEAD_DIM
    tq = min(tq, sub)
    r = ATTN_RADIUS
    per = tq // r
    nt = sub // tq
    wk, wn = w.shape
    assert wk % (n * nt) == 0
    w_spec = pl.BlockSpec((wk // (n * nt), wn), lambda b, t: (b * nt + t, 0))

    def cur(off):
        return pl.BlockSpec((None, tq, width), lambda b, t: (b, t, q_tile + off))

    def prev(off):
        return pl.BlockSpec((None, r, width), lambda b, t: (b, jnp.maximum(t * per - 1, 0), q_tile + off))

    def nxt(off):
        return pl.BlockSpec((None, r, width),
                            lambda b, t: (b, jnp.minimum((t + 1) * per, sub // r - 1), q_tile + off))

    return pl.pallas_call(
        functools.partial(_attn_kernel, tq=tq, sub=sub, n_heads=n_heads),
        grid=(n, nt),
        in_specs=[cur(0), prev(1), cur(1), nxt(1), prev(2), cur(2), nxt(2), w_spec],
        out_specs=[pl.BlockSpec((None, tq, width), lambda b, t: (b, t, 0)),
                   pl.BlockSpec((None, tq, LANES), lambda b, t: (b, t, 0)),
                   w_spec],
        out_shape=[jax.ShapeDtypeStruct((n, sub, width), BF16),
                   jax.ShapeDtypeStruct((n, sub, LANES), F32),
                   jax.ShapeDtypeStruct((wk, wn), BF16)],
        scratch_shapes=[pltpu.VMEM((tq + 2 * r, width), BF16),
                        pltpu.VMEM((tq + 2 * r, width), BF16)],
        compiler_params=_params(("parallel", "parallel")),
        name=f"attn_sub{sub}",
    )(qkv, qkv, qkv, qkv, qkv, qkv, qkv, w)


def _dft_tables(seq, gdim):
    p, fb = FFT_P, FFT_F2_BLOCK
    q = seq // p
    c = np.arange(gdim, dtype=np.float64)
    ang = 2 * np.pi * np.outer(c, c) / gdim
    wc = np.concatenate([np.cos(ang), np.sin(ang)], axis=1) / np.sqrt(gdim)
    aq = 2 * np.pi * (np.outer(np.arange(q), np.arange(q)) % q) / q
    cq, sq = np.cos(aq) / np.sqrt(q), np.sin(aq) / np.sqrt(q)
    w1 = np.stack([cq, -(sq + cq), sq - cq])
    f = (q * np.arange(p)[None, :, None, None]
         + fb * np.arange(q // fb)[:, None, None, None] + np.arange(fb)[None, None, :, None])
    phi = 2 * np.pi * ((f * np.arange(p)[None, None, None, :]) % seq) / seq
    eye = np.eye(fb)
    parts = [np.einsum('afjs,jk->afjsk', t / np.sqrt(p), eye) for t in (np.cos(phi), np.sin(phi))]
    big = np.stack(parts, axis=3).reshape(q // fb, p * fb, 2 * p * fb)
    as_bf16 = lambda a: jnp.asarray(a.astype(np.float32)).astype(BF16)
    return as_bf16(wc), as_bf16(w1), as_bf16(big)


def _fft1_kernel(u_ref, wc_ref, w1_ref, h_ref, *, gdim):
    for g in range(u_ref.shape[1] // gdim):
        sl = slice(g * gdim, (g + 1) * gdim)
        ab = jnp.dot(u_ref[:, sl], wc_ref[...], preferred_element_type=F32)
        a, b = ab[:, :gdim], ab[:, gdim:]
        k1 = jnp.dot(w1_ref[0], (a - b).astype(BF16), preferred_element_type=F32)
        k2 = jnp.dot(w1_ref[1], a.astype(BF16), preferred_element_type=F32)
        k3 = jnp.dot(w1_ref[2], b.astype(BF16), preferred_element_type=F32)
        h_ref[0, :, sl] = (k1 - k3).astype(h_ref.dtype)
        h_ref[1, :, sl] = (k1 + k2).astype(h_ref.dtype)


def _sigmoid(z):
    return 0.5 * jnp.tanh(0.5 * z) + 0.5


def _silu(z):
    h = 0.5 * z
    return h + h * jnp.tanh(h)


def _fft2_kernel(h_ref, big_ref, z_ref, o_ref):
    parts, p, fb, width = h_ref.shape
    rhs = h_ref[...].reshape(parts * p * fb, width)
    y = jnp.dot(big_ref[...], rhs, preferred_element_type=F32)
    o_ref[...] = (y.reshape(p, fb, width) * _silu(z_ref[...].astype(F32))).astype(o_ref.dtype)


def _fourier_mix(u_cm, u_tile, proj, z_tile, batch, seq, width):
    gdim = width // FOURIER_GROUPS
    p, fb = FFT_P, FFT_F2_BLOCK
    q = seq // p
    wc, w1, big = _dft_tables(seq, gdim)
    h = pl.pallas_call(
        functools.partial(_fft1_kernel, gdim=gdim),
        grid=(batch, p),
        in_specs=[pl.BlockSpec((None, None, q, width), lambda b, s: (b, s, 0, u_tile)),
                  pl.BlockSpec(wc.shape, lambda b, s: (0, 0)),
                  pl.BlockSpec(w1.shape, lambda b, s: (0, 0, 0))],
        out_specs=pl.BlockSpec((None, 2, None, q, width), lambda b, s: (b, 0, s, 0, 0)),
        out_shape=jax.ShapeDtypeStruct((batch, 2, p, q, width), BF16),
        compiler_params=_params(("parallel", "parallel")),
        name="fft_seq1",
    )(u_cm, wc, w1)
    y = pl.pallas_call(
        _fft2_kernel,
        grid=(batch, q // fb),
        in_specs=[pl.BlockSpec((None, 2, p, fb, width), lambda b, a: (b, 0, 0, a, 0)),
                  pl.BlockSpec((None, p * fb, 2 * p * fb), lambda b, a: (a, 0, 0)),
                  pl.BlockSpec((None, p, None, fb, width), lambda b, a: (b, 0, a, 0, z_tile))],
        out_specs=pl.BlockSpec((None, p, None, fb, width), lambda b, a: (b, 0, a, 0, 0)),
        out_shape=jax.ShapeDtypeStruct((batch, p, q // fb, fb, width), BF16),
        compiler_params=_params(("parallel", "parallel")),
        name="fft_seq2",
    )(h, big, proj.reshape(batch, p, q // fb, fb, proj.shape[-1]))
    return y.reshape(batch, seq, width)


PERM_T = 16


def _deinterleave_matrix(d):
    p = np.zeros((d * PERM_T, d * PERM_T), np.float32)
    for r in range(d):
        for t in range(PERM_T):
            p[t * d + r, r * PERM_T + t] = 1.0
    return jnp.asarray(p).astype(BF16)


def _branch_kernel(o1_ref, o2_ref, o3_ref, l1_ref, l2_ref, l3_ref, za_ref, af_ref, p2_ref, p3_ref,
                   wa_ref, wf_ref, ga_ref, gf_ref, ba_ref, bf_ref, out_ref,
                   a_attn, on2, on3, ln2, ln3, *, n_heads):
    def emit_tile():
        br_f = jnp.dot(af_ref[...], wf_ref[...], preferred_element_type=F32)
        br_a = jnp.dot(a_attn[...], wa_ref[...], preferred_element_type=F32)
        g_a = _sigmoid(ga_ref[...].astype(F32) + ba_ref[...])
        g_f = _sigmoid(gf_ref[...].astype(F32) + bf_ref[...])
        out_ref[...] = (g_a * br_a + g_f * br_f).astype(out_ref.dtype)

    first = pl.program_id(2) == 0

    @pl.when(first)
    def _():
        rows = ln2.shape[0]
        for src, perm_ref, dst in ((o2_ref, p2_ref, on2), (o3_ref, p3_ref, on3)):
            d = src.shape[0]
            blk = d * PERM_T
            for b in range(rows // blk):
                stacked = jnp.concatenate(
                    [src[r, b * PERM_T:(b + 1) * PERM_T, :] for r in range(d)], axis=0)
                dst[b * blk:(b + 1) * blk, :] = jnp.dot(perm_ref[...], stacked, preferred_element_type=F32)
        for src, dst in ((l2_ref, ln2), (l3_ref, ln3)):
            d = src.shape[0]
            for r in range(d):
                dst[pl.ds(r, rows // d, stride=d), :] = src[r]
        l1, l2, l3 = l1_ref[...], ln2[...], ln3[...]
        m = jnp.maximum(jnp.maximum(l1, l2), l3)
        e1, e2, e3 = jnp.exp(l1 - m), jnp.exp(l2 - m), jnp.exp(l3 - m)
        inv = 1.0 / (e1 + e2 + e3)
        al1, al2, al3 = e1 * inv, e2 * inv, e3 * inv
        for h in range(n_heads):
            sl = slice(h * HEAD_DIM, (h + 1) * HEAD_DIM)
            o = (al1[:, h:h + 1] * o1_ref[:, sl].astype(F32)
                 + al2[:, h:h + 1] * on2[:, sl]
                 + al3[:, h:h + 1] * on3[:, sl])
            a_attn[:, sl] = (o * _silu(za_ref[:, sl].astype(F32))).astype(a_attn.dtype)
        emit_tile()

    pl.when(jnp.logical_not(first))(emit_tile)


def _branches(proj, outs, lses, a_four, w_attn, w_four, gate_bias, za_tile, gate_off,
              n_heads, tm=1024, tn=512):
    batch, seq, _ = proj.shape
    aw, fw = w_attn.shape[0], w_four.shape[0]
    d = w_attn.shape[1]
    assert gate_off % tn == 0
    gate_tile = gate_off // tn
    nj = d // tn
    bias = gate_bias.reshape(1, 2 * d).astype(F32)

    def nat(width, tile=0):
        return pl.BlockSpec((None, tm, width), lambda b, i, j: (b, i, tile))

    def cm(arr):
        dil = arr.shape[1]
        return pl.BlockSpec((None, dil, tm // dil, arr.shape[3]), lambda b, i, j: (b, 0, i, 0))

    perms = [_deinterleave_matrix(o.shape[1]) for o in outs[1:]]
    whole = lambda arr: pl.BlockSpec(arr.shape, lambda b, i, j: (0, 0))

    return pl.pallas_call(
        functools.partial(_branch_kernel, n_heads=n_heads),
        grid=(batch, seq // tm, nj),
        in_specs=[nat(aw), cm(outs[1]), cm(outs[2]), nat(LANES), cm(lses[1]), cm(lses[2]),
                  nat(aw, za_tile), nat(fw), whole(perms[0]), whole(perms[1]),
                  pl.BlockSpec((aw, tn), lambda b, i, j: (0, j)),
                  pl.BlockSpec((fw, tn), lambda b, i, j: (0, j)),
                  pl.BlockSpec((None, tm, tn), lambda b, i, j: (b, i, gate_tile + j)),
                  pl.BlockSpec((None, tm, tn), lambda b, i, j: (b, i, gate_tile + nj + j)),
                  pl.BlockSpec((1, tn), lambda b, i, j: (0, j)),
                  pl.BlockSpec((1, tn), lambda b, i, j: (0, nj + j))],
        out_specs=pl.BlockSpec((None, tm, tn), lambda b, i, j: (b, i, j)),
        out_shape=jax.ShapeDtypeStruct((batch, seq, d), BF16),
        scratch_shapes=[pltpu.VMEM((tm, aw), BF16),
                        pltpu.VMEM((tm, aw), F32), pltpu.VMEM((tm, aw), F32),
                        pltpu.VMEM((tm, LANES), F32), pltpu.VMEM((tm, LANES), F32)],
        compiler_params=_params(("parallel", "parallel", "arbitrary")),
        name="branches",
    )(outs[0], outs[1], outs[2], lses[0], lses[1], lses[2], proj, a_four, perms[0], perms[1],
      w_attn, w_four, proj, proj, bias, bias)


W_SLOTS = 3


def _outproj_kernel(a_ref, w_hbm, x_ref, *refs, nj, final_norm):
    o_ref, wbuf, sem = refs[-3], refs[-2], refs[-1]
    j = pl.program_id(1)
    tn = wbuf.shape[2]
    step = pl.program_id(0) * nj + j
    n_steps = pl.num_programs(0) * nj

    def w_copy(col, slot):
        return pltpu.make_async_copy(w_hbm.at[:, col * tn:(col + 1) * tn], wbuf.at[slot], sem.at[slot])

    for jj in range(nj):
        @pl.when(j == jj)
        def _(jj=jj):
            slot = step % W_SLOTS
            if jj == 0:
                @pl.when(step == 0)
                def _():
                    for s0 in range(W_SLOTS - 1):
                        w_copy(s0 % nj, s0).start()

            @pl.when(step + W_SLOTS - 1 < n_steps)
            def _():
                w_copy((jj + W_SLOTS - 1) % nj, (step + W_SLOTS - 1) % W_SLOTS).start()

            w_copy(jj, slot).wait()
            y = x_ref[...] + jnp.dot(a_ref[...], wbuf[slot], preferred_element_type=F32)
            if not (final_norm and jj == nj - 1):
                o_ref[:, jj * tn:(jj + 1) * tn] = y
                return
            g = refs[0][...]
            done = jj * tn
            n = done + tn
            chunk = 128
            for c in range(o_ref.shape[0] // chunk):
                rows = slice(c * chunk, (c + 1) * chunk)
                v = o_ref[rows, :done]
                yr = y[rows, :]
                ss = jnp.sum(v * v, axis=-1, keepdims=True) + jnp.sum(yr * yr, axis=-1, keepdims=True)
                inv = lax.rsqrt(ss / n + NORM_EPS)
                o_ref[rows, :done] = v * inv * g[:, :done]
                o_ref[rows, done:] = yr * inv * g[:, done:]


def _outproj(a, w, x, final_gain=None, tm=512, tn=1024):
    m, k = a.shape
    n = w.shape[1]
    nj = n // tn
    operands = [a, w, x]
    in_specs = [pl.BlockSpec((tm, k), lambda i, j: (i, 0)),
                pl.BlockSpec(memory_space=pl.ANY),
                pl.BlockSpec((tm, tn), lambda i, j: (i, j))]
    if final_gain is not None:
        operands.append(final_gain.reshape(1, n).astype(F32))
        in_specs.append(pl.BlockSpec((1, n), lambda i, j: (0, 0)))
    return pl.pallas_call(
        functools.partial(_outproj_kernel, nj=nj, final_norm=final_gain is not None),
        grid=(m // tm, nj),
        in_specs=in_specs,
        out_specs=pl.BlockSpec((tm, n), lambda i, j: (i, 0)),
        out_shape=jax.ShapeDtypeStruct((m, n), F32),
        scratch_shapes=[pltpu.VMEM((W_SLOTS, k, tn), w.dtype), pltpu.SemaphoreType.DMA((W_SLOTS,))],
        compiler_params=_params(("arbitrary", "arbitrary"), vmem=VMEM_LIMIT + 2 * 1024 * 1024),
        name="outproj",
    )(*operands)


def _rotary_tables(seq):
    half = HEAD_DIM // 2
    inv_freq = ROPE_THETA ** (-jnp.arange(half, dtype=F32) * (2.0 / HEAD_DIM))
    ang = jnp.arange(seq, dtype=jnp.int32).astype(F32)[:, None] * inv_freq[None, :]
    cos, sin = jnp.cos(ang), jnp.sin(ang)
    return jnp.concatenate([cos, cos], axis=-1), jnp.concatenate([-sin, sin], axis=-1)


def kernel(x, norm_gain, w_in, gate_bias, w_branch_attn, w_branch_fourier, w_out, final_norm_gain):
    batch, seq, d = x.shape
    depth = norm_gain.shape[0]
    attn_width = w_branch_attn.shape[1]
    four_width = w_branch_fourier.shape[1]
    n_heads = attn_width // HEAD_DIM
    tn = attn_width
    assert four_width == 2 * tn and d % tn == 0
    assert w_in.shape[2] == (3 * N_GROUPS + 1) * tn + 2 * four_width + 2 * d
    assert all(w // (2 * dil) == ATTN_RADIUS for w, dil in DILATED_GROUPS)
    dilations = tuple(dil for _, dil in DILATED_GROUPS)
    assert dilations[0] == 1 and FFT_P in dilations
    q0, k0, v0 = 0, N_GROUPS, 2 * N_GROUPS
    za = 3 * N_GROUPS
    u0 = za + 1
    zf0 = u0 + 2
    g0 = zf0 + 2
    n_gate_tiles = 2 * d // tn

    cos_tab, sin_tab = _rotary_tables(seq)
    for layer in range(depth):
        *hs, w = _rmsnorm(x, norm_gain[layer], w_in[layer], dilations[1:])
        outs, lses = [], []
        a_four = None
        proj = None
        late_w = [w_branch_attn[layer], w_branch_fourier[layer], w_out[layer]]
        assert len(late_w) == len(dilations)
        late_bf = []
        for g, dil in enumerate(dilations):
            tiles = [q0 + g, k0 + g, v0 + g]
            if dil == 1:
                tiles += [za, zf0, zf0 + 1] + [g0 + t for t in range(n_gate_tiles)]
            q_tile = 0
            if dil == FFT_P:
                tiles = [u0, u0 + 1] + tiles
                q_tile = 2
            pg = _inproj(hs[g].reshape(batch * seq, d), w, tiles, q_tile, cos_tab, sin_tab, seq, dil, tn=tn)
            cols = pg.shape[1]
            o_g, lse_g, w_bf = _banded_attention(pg.reshape(batch * dil, seq // dil, cols), q_tile, n_heads,
                                                 late_w[g])
            late_bf.append(w_bf)
            if dil == 1:
                proj = pg.reshape(batch, seq, cols)
                outs.append(o_g)
                lses.append(lse_g)
            else:
                outs.append(o_g.reshape(batch, dil, seq // dil, attn_width))
                lses.append(lse_g.reshape(batch, dil, seq // dil, LANES))
            if dil == FFT_P:
                a_four = _fourier_mix(pg.reshape(batch, dil, seq // dil, cols), 0, proj, 2,
                                      batch, seq, four_width)
        mixed = _branches(proj, outs, lses, a_four, late_bf[0], late_bf[1], gate_bias[layer],
                          3, 6 * tn, n_heads)
        x = _outproj(mixed.reshape(batch * seq, d), late_bf[2], x.reshape(batch * seq, d),
                     final_norm_gain if layer == depth - 1 else None).reshape(batch, seq, d)
    return x
```
